```python
import jax, jax.numpy as jnp
from jax import lax
import numpy as np

D_MODEL = 4096
BATCH = 4
SEQ = 2048
DEPTH = 2

CTX_LEN = 256
GRID_W = 64
HEAD_DIM = 128
MIX_WIDTH = D_MODEL
A_WIDTH = MIX_WIDTH // 2
A_HEADS = A_WIDTH // HEAD_DIM
A_KV_HEADS = A_HEADS // 4
A_GROUP = A_HEADS // A_KV_HEADS
KV_WIDTH = A_KV_HEADS * HEAD_DIM
WINDOW = 128
ATTN_BLOCK = 128
B_WIDTH = MIX_WIDTH - A_WIDTH
B_GROUPS = B_WIDTH // HEAD_DIM
CHUNK = 128
EVEN_SPLITS = (A_WIDTH, A_WIDTH + KV_WIDTH, A_WIDTH + 2 * KV_WIDTH, A_WIDTH + 2 * KV_WIDTH + B_WIDTH)
EVEN_IN = A_WIDTH + 2 * KV_WIDTH + 2 * B_WIDTH
C_WIDTH = MIX_WIDTH // 2
CONV_WIDTH = 31
D_WIDTH = MIX_WIDTH - C_WIDTH
SHORT_WIDTH = 3
ODD_SPLITS = (C_WIDTH, 2 * C_WIDTH, 2 * C_WIDTH + D_WIDTH, 2 * C_WIDTH + 2 * D_WIDTH)
ODD_IN = 2 * C_WIDTH + 3 * D_WIDTH
N_EXPERTS = 64
TOP_K = 6
N_GROUPS = 8
TOPK_GROUPS = 4
D_EXPERT = D_MODEL * 3 // 32
D_SHARED = 2 * D_EXPERT
ROUTED_SCALE = 2.5
MOE_BLOCK = 1024
ROPE_THETA = 10000.0
NORM_EPS = 1e-6
N_EVEN = (DEPTH + 1) // 2
N_ODD = DEPTH // 2

kernel_name = 'hybrid_window_gmlp_conformer_moe_dit'


def rms_norm(x, g):
    xf = x.astype(jnp.float32)
    y = xf * lax.rsqrt(jnp.mean(xf * xf, axis=-1, keepdims=True) + NORM_EPS)
    return (y * g.astype(jnp.float32)).astype(x.dtype)


def axial_rope_tables(n_tok, dtype):
    rows = n_tok // GRID_W
    row = jnp.repeat(jnp.arange(rows), GRID_W)
    col = jnp.tile(jnp.arange(GRID_W), rows)
    pos = jnp.stack([row, col], axis=-1).astype(jnp.float32)
    n_freq = HEAD_DIM // 4
    inv_freq = ROPE_THETA ** (-jnp.arange(n_freq, dtype=jnp.float32) / n_freq)
    ang = pos[:, :, None] * inv_freq
    ang = jnp.broadcast_to(ang[:, :, None, :], (n_tok, 2, 2, n_freq)).reshape(n_tok, HEAD_DIM)
    return jnp.cos(ang).astype(dtype), jnp.sin(ang).astype(dtype)


def apply_rope(x, cos, sin):
    xs = x.reshape(x.shape[:-1] + (2, 2, HEAD_DIM // 4))
    rot = jnp.concatenate([-xs[..., 1:2, :], xs[..., 0:1, :]], axis=-2).reshape(x.shape)
    return x * cos[:, None, :] + rot * sin[:, None, :]


def band_mask(n_tok):
    nb = n_tok // ATTN_BLOCK
    qi = jnp.arange(ATTN_BLOCK)[:, None]
    si = jnp.arange(3 * ATTN_BLOCK)[None, :]
    rel = si - ATTN_BLOCK - qi
    kpos = jnp.arange(nb)[:, None, None] * ATTN_BLOCK + si[None] - ATTN_BLOCK
    return (jnp.abs(rel) <= WINDOW)[None] & (kpos >= 0) & (kpos < n_tok)


def window_blocks(t):
    b, s = t.shape[:2]
    nb = s // ATTN_BLOCK
    tp = jnp.pad(t, ((0, 0), (ATTN_BLOCK, ATTN_BLOCK), (0, 0), (0, 0)))
    tp = tp.reshape((b, nb + 2, ATTN_BLOCK) + t.shape[2:])
    return jnp.concatenate([tp[:, :-2], tp[:, 1:-1], tp[:, 2:]], axis=2)


def latent_window_attention(q, k, v, kc, vc, sink):
    b, s = q.shape[:2]
    nb = s // ATTN_BLOCK
    n_loc = 3 * ATTN_BLOCK
    n_ctx = kc.shape[1]
    scale = HEAD_DIM ** -0.5
    qb = q.reshape(b, nb, ATTN_BLOCK, A_KV_HEADS, A_GROUP, HEAD_DIM)
    kw = window_blocks(k)
    vw = window_blocks(v)
    s_loc = jnp.einsum('bnqkgd,bnskd->bnkgqs', qb, kw).astype(jnp.float32) * scale
    s_loc = jnp.where(band_mask(s)[None, :, None, None], s_loc, -jnp.inf)
    s_ctx = jnp.einsum('bnqkgd,bckd->bnkgqc', qb, kc).astype(jnp.float32) * scale
    s_sink = jnp.broadcast_to(sink.astype(jnp.float32).reshape(A_KV_HEADS, A_GROUP, 1, 1), s_loc.shape[:-1] + (1,))
    p = jax.nn.softmax(jnp.concatenate([s_loc, s_ctx, s_sink], axis=-1), axis=-1).astype(q.dtype)
    out = (jnp.einsum('bnkgqs,bnskd->bnqkgd', p[..., :n_loc], vw)
           + jnp.einsum('bnkgqc,bckd->bnqkgd', p[..., n_loc:n_loc + n_ctx], vc))
    return out.reshape(b, s, A_WIDTH)


def context_attention(q, k, v, sink):
    b, n = q.shape[:2]
    scale = HEAD_DIM ** -0.5
    qg = q.reshape(b, n, A_KV_HEADS, A_GROUP, HEAD_DIM)
    sc = jnp.einsum('bqkgd,bckd->bkgqc', qg, k).astype(jnp.float32) * scale
    s_sink = jnp.broadcast_to(sink.astype(jnp.float32).reshape(A_KV_HEADS, A_GROUP, 1, 1), sc.shape[:-1] + (1,))
    p = jax.nn.softmax(jnp.concatenate([sc, s_sink], axis=-1), axis=-1)[..., :-1].astype(q.dtype)
    return jnp.einsum('bkgqc,bckd->bqkgd', p, v).reshape(b, n, A_WIDTH)


def spatial_gating(u, vg, gate_norm, spatial_w, spatial_b):
    u = jax.nn.gelu(u)
    vg = rms_norm(jax.nn.gelu(vg), gate_norm)
    b, n = u.shape[:2]
    vc = vg.reshape(b, n // CHUNK, CHUNK, B_GROUPS, HEAD_DIM)
    mixed = jnp.einsum('gpq,bnqgc->bnpgc', spatial_w, vc) + spatial_b.T[None, None, :, :, None]
    return u * mixed.reshape(b, n, B_WIDTH)


def even_mixer(h, hc, w_in, w_out, sink, gate_norm, spatial_w, spatial_b, cos, sin, with_ctx):
    b, s, _ = h.shape
    n_ctx = hc.shape[1]
    q, k, v, u, vg = jnp.split(h @ w_in, EVEN_SPLITS, axis=-1)
    q = apply_rope(q.reshape(b, s, A_HEADS, HEAD_DIM), cos, sin)
    k = apply_rope(k.reshape(b, s, A_KV_HEADS, HEAD_DIM), cos, sin)
    v = v.reshape(b, s, A_KV_HEADS, HEAD_DIM)
    if with_ctx:
        qc, kc, vc, uc, vgc = jnp.split(hc @ w_in, EVEN_SPLITS, axis=-1)
    else:
        kc, vc = jnp.split(hc @ w_in[:, A_WIDTH:A_WIDTH + 2 * KV_WIDTH], 2, axis=-1)
    kc = kc.reshape(b, n_ctx, A_KV_HEADS, HEAD_DIM)
    vc = vc.reshape(b, n_ctx, A_KV_HEADS, HEAD_DIM)
    y = jnp.concatenate([latent_window_attention(q, k, v, kc, vc, sink),
                         spatial_gating(u, vg, gate_norm, spatial_w, spatial_b)], axis=-1) @ w_out
    if not with_ctx:
        return y, None
    yc = jnp.concatenate([context_attention(qc.reshape(b, n_ctx, A_HEADS, HEAD_DIM), kc, vc, sink),
                          spatial_gating(uc, vgc, gate_norm, spatial_w, spatial_b)], axis=-1) @ w_out
    return y, yc


def depthwise_conv(x, w):
    pad = w.shape[0] // 2
    return lax.conv_general_dilated(x, w[:, None, :], window_strides=(1,), padding=[(pad, pad)],
                                    dimension_numbers=('NWC', 'WIO', 'NWC'), feature_group_count=x.shape[-1])


def conv_mixers(h, w_in, w_out, dw_w, dw_b, conv_norm, short_w):
    a, g, gb, gc, hx = jnp.split(h @ w_in, ODD_SPLITS, axis=-1)
    yc = depthwise_conv(a * jax.nn.sigmoid(g), dw_w) + dw_b
    yc = jax.nn.silu(rms_norm(yc, conv_norm))
    yd = gb * depthwise_conv(gc * hx, short_w)
    return jnp.concatenate([yc, yd], axis=-1) @ w_out


def moe_ffn(h, router_w, router_b, w_gate, w_up, w_down, s_gate, s_up, s_down):
    n, d = h.shape
    scores = jax.nn.sigmoid((h @ router_w).astype(jnp.float32))
    biased = scores + router_b.astype(jnp.float32)
    grouped = biased.reshape(n, N_GROUPS, N_EXPERTS // N_GROUPS)
    group_score = lax.top_k(grouped, 2)[0].sum(-1)
    top_groups = lax.top_k(group_score, TOPK_GROUPS)[1]
    group_mask = jnp.sum(jax.nn.one_hot(top_groups, N_GROUPS, dtype=jnp.float32), axis=-2) > 0
    expert_mask = jnp.repeat(group_mask, N_EXPERTS // N_GROUPS, axis=-1)
    top_idx = lax.top_k(jnp.where(expert_mask, biased, -jnp.inf), TOP_K)[1]
    wts = jnp.take_along_axis(scores, top_idx, axis=-1)
    wts = wts / jnp.sum(wts, axis=-1, keepdims=True) * ROUTED_SCALE
    gates = jnp.einsum('nk,nke->ne', wts, jax.nn.one_hot(top_idx, N_EXPERTS, dtype=jnp.float32))
    pad = (-n) % MOE_BLOCK
    hb = jnp.pad(h, ((0, pad), (0, 0))).reshape(-1, MOE_BLOCK, d)
    gb = jnp.pad(gates, ((0, pad), (0, 0))).reshape(-1, MOE_BLOCK, N_EXPERTS).astype(h.dtype)

    def token_block(args):
        xt, gt = args
        hid = jax.nn.silu(jnp.einsum('td,edf->tef', xt, w_gate)) * jnp.einsum('td,edf->tef', xt, w_up)
        return jnp.einsum('tef,efd->td', hid * gt[:, :, None], w_down)

    routed = lax.map(token_block, (hb, gb)).reshape(-1, d)[:n]
    shared = (jax.nn.silu(h @ s_gate) * (h @ s_up)) @ s_down
    return routed + shared


def setup_inputs(seed: int = 0) -> dict:
    key = jax.random.key(seed)
    ks = jax.random.split(key, 32)
    f32 = jnp.float32

    def nrm(k, shape, fan_in):
        return jax.random.normal(k, shape, f32) * (fan_in ** -0.5)

    def gain(k, shape):
        return 1.0 + 0.02 * jax.random.normal(k, shape, f32)

    D = D_MODEL
    return {
        'x': jax.random.normal(ks[0], (BATCH, SEQ, D), f32),
        'c': jax.random.normal(ks[1], (BATCH, D), f32),
        'ctx': jax.random.normal(ks[2], (BATCH, CTX_LEN, D), f32),
        'c_ctx': jax.random.normal(ks[3], (D,), f32),
        'ada_w': 0.5 * nrm(ks[4], (DEPTH, D, 6 * D), D),
        'ada_b': 0.02 * jax.random.normal(ks[5], (DEPTH, 6 * D), f32),
        'norm_mix': gain(ks[6], (DEPTH, D)),
        'norm_ffn': gain(ks[7], (DEPTH, D)),
        'ev_w_in': nrm(ks[8], (N_EVEN, D, EVEN_IN), D),
        'ev_w_out': nrm(ks[9], (N_EVEN, MIX_WIDTH, D), MIX_WIDTH),
        'ev_sink': 0.5 * jax.random.normal(ks[10], (N_EVEN, A_HEADS), f32),
        'ev_gate_norm': gain(ks[11], (N_EVEN, B_WIDTH)),
        'ev_spatial_w': nrm(ks[12], (N_EVEN, B_GROUPS, CHUNK, CHUNK), CHUNK),
        'ev_spatial_b': 1.0 + 0.1 * jax.random.normal(ks[13], (N_EVEN, B_GROUPS, CHUNK), f32),
        'od_w_in': nrm(ks[14], (N_ODD, D, ODD_IN), D),
        'od_w_out': nrm(ks[15], (N_ODD, MIX_WIDTH, D), MIX_WIDTH),
        'od_dw_w': nrm(ks[16], (N_ODD, CONV_WIDTH, C_WIDTH), CONV_WIDTH),
        'od_dw_b': 0.02 * jax.random.normal(ks[17], (N_ODD, C_WIDTH), f32),
        'od_conv_norm': gain(ks[18], (N_ODD, C_WIDTH)),
        'od_short_w': nrm(ks[19], (N_ODD, SHORT_WIDTH, D_WIDTH), SHORT_WIDTH),
        'router_w': nrm(ks[20], (DEPTH, D, N_EXPERTS), D),
        'router_b': 0.01 * jax.random.normal(ks[21], (DEPTH, N_EXPERTS), f32),
        'exp_w_gate': nrm(ks[22], (DEPTH, N_EXPERTS, D, D_EXPERT), D),
        'exp_w_up': nrm(ks[23], (DEPTH, N_EXPERTS, D, D_EXPERT), D),
        'exp_w_down': nrm(ks[24], (DEPTH, N_EXPERTS, D_EXPERT, D), D_EXPERT),
        'sh_w_gate': nrm(ks[25], (DEPTH, D, D_SHARED), D),
        'sh_w_up': nrm(ks[26], (DEPTH, D, D_SHARED), D),
        'sh_w_down': nrm(ks[27], (DEPTH, D_SHARED, D), D_SHARED),
        'final_norm': gain(ks[28], (D,)),
    }


def reference(x, c, ctx, c_ctx, ada_w, ada_b, norm_mix, norm_ffn, ev_w_in, ev_w_out, ev_sink, ev_gate_norm,
              ev_spatial_w, ev_spatial_b, od_w_in, od_w_out, od_dw_w, od_dw_b, od_conv_norm, od_short_w,
              router_w, router_b, exp_w_gate, exp_w_up, exp_w_down, sh_w_gate, sh_w_up, sh_w_down, final_norm):
    b, s, d = x.shape
    n_ctx = ctx.shape[1]
    cos, sin = axial_rope_tables(s, x.dtype)
    silu_c = jax.nn.silu(c)
    silu_cc = jax.nn.silu(c_ctx)
    for i in range(DEPTH):
        j = i // 2
        with_ctx = any(l % 2 == 0 for l in range(i + 1, DEPTH))
        need_ctx_in = with_ctx or (i % 2 == 0)
        mod = silu_c @ ada_w[i] + ada_b[i]
        mod_c = silu_cc @ ada_w[i] + ada_b[i]
        sh1, sc1, g1, sh2, sc2, g2 = jnp.split(mod[:, None, :], 6, axis=-1)
        csh1, csc1, cg1, csh2, csc2, cg2 = jnp.split(mod_c, 6, axis=-1)
        h = rms_norm(x, norm_mix[i]) * (1 + sc1) + sh1
        hc = rms_norm(ctx, norm_mix[i]) * (1 + csc1) + csh1 if need_ctx_in else None
        if i % 2 == 0:
            y, yc = even_mixer(h, hc, ev_w_in[j], ev_w_out[j], ev_sink[j], ev_gate_norm[j], ev_spatial_w[j],
                               ev_spatial_b[j], cos, sin, with_ctx)
        else:
            y = conv_mixers(h, od_w_in[j], od_w_out[j], od_dw_w[j], od_dw_b[j], od_conv_norm[j], od_short_w[j])
            yc = (conv_mixers(hc, od_w_in[j], od_w_out[j], od_dw_w[j], od_dw_b[j], od_conv_norm[j], od_short_w[j])
                  if with_ctx else None)
        x = x + g1 * y
        hf = (rms_norm(x, norm_ffn[i]) * (1 + sc2) + sh2).reshape(b * s, d)
        if with_ctx:
            ctx = ctx + cg1 * yc
            hfc = (rms_norm(ctx, norm_ffn[i]) * (1 + csc2) + csh2).reshape(b * n_ctx, d)
            hf = jnp.concatenate([hf, hfc], axis=0)
        f = moe_ffn(hf, router_w[i], router_b[i], exp_w_gate[i], exp_w_up[i], exp_w_down[i],
                    sh_w_gate[i], sh_w_up[i], sh_w_down[i])
        x = x + g2 * f[:b * s].reshape(b, s, d)
        if with_ctx:
            ctx = ctx + cg2 * f[b * s:].reshape(b, n_ctx, d)
    return rms_norm(x, final_norm)
```

```python
import functools

import jax
import jax.numpy as jnp
from jax import lax
from jax.experimental import pallas as pl
from jax.experimental.pallas import tpu as pltpu

F32 = jnp.float32
BF16 = jnp.bfloat16

HEAD_DIM = 128
ATTN_BLOCK = 128
WINDOW = 128
CHUNK = 128
KV_HEADS = 4
Q_GROUP = 4
CONV_WIDTH = 31
SHORT_WIDTH = 3
CONV_HALO = 16
N_EXPERTS = 64
N_GROUPS = 8
TOPK_GROUPS = 4
TOP_K = 6
ROUTED_SCALE = 2.5
ROPE_THETA = 10000.0
NORM_EPS = 1e-6
GRID_W = 64
VMEM_LIMIT = 56 * 1024 * 1024


def _params(*sem):
    return pltpu.CompilerParams(dimension_semantics=sem, vmem_limit_bytes=VMEM_LIMIT)


def _silu(x):
    return x * jax.nn.sigmoid(x)


def _ada_kernel(c_ref, w_ref, b_ref, o_ref):
    acc = jnp.dot(c_ref[...], w_ref[...].astype(BF16), preferred_element_type=F32)
    o_ref[...] = acc + b_ref[...]


def ada_modulation(c_rows, ada_w, ada_b, tn=512):
    depth, d, n = ada_w.shape
    r = c_rows.shape[0]
    return pl.pallas_call(
        _ada_kernel,
        grid=(depth, n // tn),
        in_specs=[
            pl.BlockSpec((r, d), lambda l, j: (0, 0)),
            pl.BlockSpec((None, d, tn), lambda l, j: (l, 0, j)),
            pl.BlockSpec((None, 1, tn), lambda l, j: (l, 0, j)),
        ],
        out_specs=pl.BlockSpec((None, r, tn), lambda l, j: (l, 0, j)),
        out_shape=jax.ShapeDtypeStruct((depth, r, n), F32),
        compiler_params=_params("parallel", "parallel"),
        name="ada_modulation",
    )(c_rows, ada_w, ada_b.reshape(depth, 1, n))


def _modnorm_rows(x, g, sc1p, sh):
    ms = jnp.mean(x * x, axis=-1, keepdims=True)
    return (x * lax.rsqrt(ms + NORM_EPS)) * g * sc1p + sh


def _norm_matmul_kernel(x_ref, g_ref, sc_ref, sh_ref, w_ref, o_ref, h_ref, *, rows):
    @pl.when(pl.program_id(1) == 0)
    def _():
        def body(r, carry):
            rs = pl.ds(pl.multiple_of(r * rows, rows), rows)
            h_ref[rs, :] = _modnorm_rows(x_ref[rs, :], g_ref[...], sc_ref[...], sh_ref[...]).astype(BF16)
            return carry
        lax.fori_loop(0, x_ref.shape[0] // rows, body, 0)

    o_ref[...] = jnp.dot(h_ref[...], w_ref[...], preferred_element_type=F32).astype(o_ref.dtype)


def norm_matmul(x, g, sc1p, sh, w, seq, *, tm=512, tn=512, out_dtype=BF16, name="norm_matmul"):
    m, d = x.shape
    n = w.shape[1]
    tm = min(tm, m)
    tn = min(tn, n)
    assert m % tm == 0 and n % tn == 0 and seq % tm == 0
    rows = min(128, tm)
    return pl.pallas_call(
        functools.partial(_norm_matmul_kernel, rows=rows),
        grid=(m // tm, n // tn),
        in_specs=[
            pl.BlockSpec((tm, d), lambda i, j: (i, 0)),
            pl.BlockSpec((1, d), lambda i, j: (0, 0)),
            pl.BlockSpec((None, 1, d), lambda i, j: ((i * tm) // seq, 0, 0)),
            pl.BlockSpec((None, 1, d), lambda i, j: ((i * tm) // seq, 0, 0)),
            pl.BlockSpec((d, tn), lambda i, j: (0, j)),
        ],
        out_specs=pl.BlockSpec((tm, tn), lambda i, j: (i, j)),
        out_shape=jax.ShapeDtypeStruct((m, n), out_dtype),
        scratch_shapes=[pltpu.VMEM((tm, d), BF16)],
        compiler_params=_params("parallel", "arbitrary"),
        name=name,
    )(x, g, sc1p, sh, w)


def _rope_kernel(x_ref, cos_ref, sa_ref, sb_ref, o_ref, *, heads):
    cos, sa, sb = cos_ref[...], sa_ref[...], sb_ref[...]
    for h in range(heads):
        cs = slice(h * HEAD_DIM, (h + 1) * HEAD_DIM)
        x = x_ref[:, cs].astype(F32)
        y = x * cos + pltpu.roll(x, 3 * HEAD_DIM // 4, 1) * sa + pltpu.roll(x, HEAD_DIM // 4, 1) * sb
        o_ref[:, cs] = y.astype(o_ref.dtype)


def rope_tables(seq):
    rows = seq // GRID_W
    row = jnp.repeat(jnp.arange(rows), GRID_W)
    col = jnp.tile(jnp.arange(GRID_W), rows)
    pos = jnp.stack([row, col], axis=-1).astype(F32)
    n_freq = HEAD_DIM // 4
    inv_freq = ROPE_THETA ** (-jnp.arange(n_freq, dtype=F32) / n_freq)
    ang = pos[:, :, None] * inv_freq
    ang = jnp.broadcast_to(ang[:, :, None, :], (seq, 2, 2, n_freq)).reshape(seq, HEAD_DIM)
    cos, sin = jnp.cos(ang), jnp.sin(ang)
    first = (jnp.arange(HEAD_DIM) // n_freq) % 2 == 0
    return cos, jnp.where(first, -sin, 0.0), jnp.where(first, 0.0, sin)


def apply_rope(proj, tables, seq, width, tm=128):
    m = proj.shape[0]
    nblk = seq // tm
    tspec = pl.BlockSpec((tm, HEAD_DIM), lambda i: (i % nblk, 0))
    return pl.pallas_call(
        functools.partial(_rope_kernel, heads=width // HEAD_DIM),
        grid=(m // tm,),
        in_specs=[pl.BlockSpec((tm, width), lambda i: (i, 0)), tspec, tspec, tspec],
        out_specs=pl.BlockSpec((tm, width), lambda i: (i, 0)),
        out_shape=jax.ShapeDtypeStruct((m, width), proj.dtype),
        compiler_params=_params("parallel"),
        name="rope",
    )(proj, *tables)


def _attn_kernel(sink_ref, q_ref, kp_ref, kc_ref, kn_ref, vp_ref, vc_ref, vn_ref, kx_ref, vx_ref, o_ref, *, nb):
    n = pl.program_id(1)
    kv = pl.program_id(2)
    blk = ATTN_BLOCK
    q = q_ref[...]
    qs = jnp.concatenate([q[:, g * HEAD_DIM:(g + 1) * HEAD_DIM] for g in range(Q_GROUP)], axis=0)
    keys = jnp.concatenate([kp_ref[...], kc_ref[...], kn_ref[...], kx_ref[...]], axis=0)
    vals = jnp.concatenate([vp_ref[...], vc_ref[...], vn_ref[...], vx_ref[...]], axis=0)
    s = lax.dot_general(qs, keys, (((1,), (1,)), ((), ())), preferred_element_type=F32) * (HEAD_DIM ** -0.5)
    rows, cols = s.shape
    qi = lax.broadcasted_iota(jnp.int32, (rows, cols), 0) % blk
    si = lax.broadcasted_iota(jnp.int32, (rows, cols), 1)
    rel = si - blk - qi
    kpos = n * blk + si - blk
    local_ok = (jnp.abs(rel) <= WINDOW) & (kpos >= 0) & (kpos < nb * blk)
    s = jnp.where((si >= 3 * blk) | local_ok, s, -jnp.inf)
    sink = jnp.concatenate(
        [jnp.full((blk, 1), sink_ref[kv * Q_GROUP + g], F32) for g in range(Q_GROUP)], axis=0)
    mx = jnp.maximum(jnp.max(s, axis=-1, keepdims=True), sink)
    p = jnp.exp(s - mx)
    denom = jnp.sum(p, axis=-1, keepdims=True) + jnp.exp(sink - mx)
    o = jnp.dot(p.astype(vals.dtype), vals, preferred_element_type=F32) / denom
    for g in range(Q_GROUP):
        o_ref[:, g * HEAD_DIM:(g + 1) * HEAD_DIM] = o[g * blk:(g + 1) * blk, :].astype(o_ref.dtype)


def window_attention(qk, proj, kvx, sink, batch, seq, n_ctx):
    nb = seq // ATTN_BLOCK
    m = batch * seq
    q_cols = Q_GROUP * HEAD_DIM
    k_off = KV_HEADS * Q_GROUP
    v_off = k_off + KV_HEADS

    def rows(shift):
        return lambda b, n, k, s_: b * nb + jnp.clip(n + shift, 0, nb - 1)

    def kspec(shift, off):
        r = rows(shift)
        return pl.BlockSpec((ATTN_BLOCK, HEAD_DIM), lambda b, n, k, s_: (r(b, n, k, s_), off + k))

    grid_spec = pltpu.PrefetchScalarGridSpec(
        num_scalar_prefetch=1,
        grid=(batch, nb, KV_HEADS),
        in_specs=[
            pl.BlockSpec((ATTN_BLOCK, q_cols), lambda b, n, k, s_: (b * nb + n, k)),
            kspec(-1, k_off), kspec(0, k_off), kspec(1, k_off),
            kspec(-1, v_off), kspec(0, v_off), kspec(1, v_off),
            pl.BlockSpec((n_ctx, HEAD_DIM), lambda b, n, k, s_: (b, k)),
            pl.BlockSpec((n_ctx, HEAD_DIM), lambda b, n, k, s_: (b, KV_HEADS + k)),
        ],
        out_specs=pl.BlockSpec((ATTN_BLOCK, q_cols), lambda b, n, k, s_: (b * nb + n, k)),
    )
    return pl.pallas_call(
        functools.partial(_attn_kernel, nb=nb),
        grid_spec=grid_spec,
        out_shape=jax.ShapeDtypeStruct((m, KV_HEADS * q_cols), BF16),
        compiler_params=_params("parallel", "parallel", "parallel"),
        name="window_attention",
    )(sink, qk, qk, qk, qk, proj, proj, proj, kvx, kvx)


def _gating_kernel(u0_ref, u1_ref, v0_ref, v1_ref, gn_ref, w_ref, b_ref, o_ref, *, groups):
    half = groups // 2
    v = jnp.concatenate([v0_ref[...], v1_ref[...]], axis=-1).astype(F32)
    v = jax.nn.gelu(v)
    ms = jnp.mean(v * v, axis=-1, keepdims=True)
    vn = ((v * lax.rsqrt(ms + NORM_EPS)) * gn_ref[...]).astype(BF16)
    for g in range(groups):
        cs = slice(g * HEAD_DIM, (g + 1) * HEAD_DIM)
        mixed = jnp.dot(w_ref[g], vn[:, cs], preferred_element_type=F32) + b_ref[:, g:g + 1]
        u_ref = u0_ref if g < half else u1_ref
        us = slice((g % half) * HEAD_DIM, (g % half + 1) * HEAD_DIM)
        o_ref[:, cs] = (jax.nn.gelu(u_ref[:, us].astype(F32)) * mixed).astype(o_ref.dtype)


def spatial_gating(proj, gate_norm, spatial_w, spatial_b, u_col, width):
    m = proj.shape[0]
    groups = width // HEAD_DIM
    hw = width // 2
    base = u_col // hw
    assert u_col % hw == 0

    def col(k):
        return pl.BlockSpec((CHUNK, hw), lambda i: (i, base + k))

    return pl.pallas_call(
        functools.partial(_gating_kernel, groups=groups),
        grid=(m // CHUNK,),
        in_specs=[
            col(0), col(1), col(2), col(3),
            pl.BlockSpec((1, width), lambda i: (0, 0)),
            pl.BlockSpec((groups, CHUNK, CHUNK), lambda i: (0, 0, 0)),
            pl.BlockSpec((CHUNK, groups), lambda i: (0, 0)),
        ],
        out_specs=pl.BlockSpec((CHUNK, width), lambda i: (i, 0)),
        out_shape=jax.ShapeDtypeStruct((m, width), BF16),
        compiler_params=_params("parallel"),
        name="spatial_gating",
    )(proj, proj, proj, proj, gate_norm.reshape(1, width), spatial_w.astype(BF16), spatial_b.T)


def _resid_matmul_kernel(*refs, n_pairs, has_extra):
    a_refs = refs[:n_pairs]
    w_refs = refs[n_pairs:2 * n_pairs]
    pos = 2 * n_pairs
    extra_ref = refs[pos] if has_extra else None
    pos += int(has_extra)
    res_ref, gate_ref, o_ref = refs[pos], refs[pos + 1], refs[pos + 2]
    acc = jnp.dot(a_refs[0][...], w_refs[0][...], preferred_element_type=F32)
    for a_ref, w_ref in zip(a_refs[1:], w_refs[1:]):
        acc += jnp.dot(a_ref[...], w_ref[...], preferred_element_type=F32)
    if has_extra:
        acc += extra_ref[...].astype(F32)
    o_ref[...] = res_ref[...] + gate_ref[...] * acc


def resid_matmul(a_list, w_list, res, gate, seq, extra=None, *, tm=512, tn=512, name="resid_matmul"):
    m, n = res.shape
    tm = min(tm, m)
    assert m % tm == 0 and n % tn == 0 and seq % tm == 0
    in_specs = [pl.BlockSpec((tm, a.shape[1]), lambda i, j: (i, 0)) for a in a_list]
    in_specs += [pl.BlockSpec((w.shape[0], tn), lambda i, j: (0, j)) for w in w_list]
    args = list(a_list) + list(w_list)
    if extra is not None:
        in_specs.append(pl.BlockSpec((tm, tn), lambda i, j: (i, j)))
        args.append(extra)
    in_specs += [
        pl.BlockSpec((tm, tn), lambda i, j: (i, j)),
        pl.BlockSpec((None, 1, tn), lambda i, j: ((i * tm) // seq, 0, j)),
    ]
    args += [res, gate]
    return pl.pallas_call(
        functools.partial(_resid_matmul_kernel, n_pairs=len(a_list), has_extra=extra is not None),
        grid=(m // tm, n // tn),
        in_specs=in_specs,
        out_specs=pl.BlockSpec((tm, tn), lambda i, j: (i, j)),
        out_shape=jax.ShapeDtypeStruct((m, n), F32),
        compiler_params=_params("parallel", "parallel"),
        name=name,
    )(*args)


def _conv_kernel(a_ref, ap_ref, an_ref, g_ref, gp_ref, gn_ref, gb_ref,
                 c_ref, cp_ref, cn_ref, x_ref, xp_ref, xn_ref,
                 dw_ref, db_ref, cg_ref, sw_ref, yc_ref, yd_ref, z_ref, y_ref, ssq_ref, *, tiles_per_seq):
    tp, width = a_ref.shape
    halo = CONV_HALO
    t = pl.program_id(0) % tiles_per_seq
    keep_prev = (t > 0).astype(F32)
    keep_next = (t < tiles_per_seq - 1).astype(F32)
    lane_chunks = width // 128

    def glu(a, g):
        return a.astype(F32) * jax.nn.sigmoid(g.astype(F32))

    def prod(a, b):
        return a.astype(F32) * b.astype(F32)

    def fill(main, prev, nxt):
        z_ref[pl.ds(0, halo), :] = prev * keep_prev
        z_ref[pl.ds(halo, tp), :] = main
        z_ref[pl.ds(halo + tp, halo), :] = nxt * keep_next

    def conv(w_ref, taps, cs):
        pad = taps // 2
        acc = jnp.zeros((tp, 128), F32)
        for k in range(taps):
            acc = acc + z_ref[pl.ds(halo + k - pad, tp), cs] * w_ref[pl.ds(k, 1), cs]
        return acc

    fill(glu(a_ref[...], g_ref[...]), glu(ap_ref[...], gp_ref[...]), glu(an_ref[...], gn_ref[...]))
    ssq_ref[...] = jnp.zeros_like(ssq_ref)

    def conv_c(c, carry):
        cs = pl.ds(pl.multiple_of(c * 128, 128), 128)
        y = conv(dw_ref, CONV_WIDTH, cs) + db_ref[:, cs]
        y_ref[:, cs] = y
        ssq_ref[...] += y * y
        return carry
    lax.fori_loop(0, lane_chunks, conv_c, 0)
    inv = lax.rsqrt(jnp.sum(ssq_ref[...], axis=-1, keepdims=True) / width + NORM_EPS)

    def norm_c(c, carry):
        cs = pl.ds(pl.multiple_of(c * 128, 128), 128)
        yc_ref[:, cs] = _silu(y_ref[:, cs] * inv * cg_ref[:, cs]).astype(yc_ref.dtype)
        return carry
    lax.fori_loop(0, lane_chunks, norm_c, 0)

    fill(prod(c_ref[...], x_ref[...]), prod(cp_ref[...], xp_ref[...]), prod(cn_ref[...], xn_ref[...]))

    def conv_d(c, carry):
        cs = pl.ds(pl.multiple_of(c * 128, 128), 128)
        yd_ref[:, cs] = (gb_ref[:, cs].astype(F32) * conv(sw_ref, SHORT_WIDTH, cs)).astype(yd_ref.dtype)
        return carry
    lax.fori_loop(0, lane_chunks, conv_d, 0)


def conv_mixers(proj, dw_w, dw_b, conv_norm, short_w, seq, *, tp=256):
    m = proj.shape[0]
    width = proj.shape[1] // 5
    tp = min(tp, seq)
    assert seq % tp == 0 and tp % CONV_HALO == 0
    per = tp // CONV_HALO
    n_halo = m // CONV_HALO

    def main(cb):
        return pl.BlockSpec((tp, width), lambda i: (i, cb))

    def prev(cb):
        return pl.BlockSpec((CONV_HALO, width), lambda i: (jnp.maximum(i * per - 1, 0), cb))

    def nxt(cb):
        return pl.BlockSpec((CONV_HALO, width), lambda i: (jnp.minimum((i + 1) * per, n_halo - 1), cb))

    def full(arr):
        return pl.BlockSpec(arr.shape, lambda i: (0, 0))

    dw_b2, cn2 = dw_b.reshape(1, width), conv_norm.reshape(1, width)
    out = jax.ShapeDtypeStruct((m, width), BF16)
    return pl.pallas_call(
        functools.partial(_conv_kernel, tiles_per_seq=seq // tp),
        grid=(m // tp,),
        in_specs=[main(0), prev(0), nxt(0), main(1), prev(1), nxt(1), main(2),
                  main(3), prev(3), nxt(3), main(4), prev(4), nxt(4),
                  full(dw_w), full(dw_b2), full(cn2), full(short_w)],
        out_specs=[pl.BlockSpec((tp, width), lambda i: (i, 0))] * 2,
        out_shape=[out, out],
        scratch_shapes=[pltpu.VMEM((tp + 2 * CONV_HALO, width), F32), pltpu.VMEM((tp, width), F32),
                        pltpu.VMEM((tp, 128), F32)],
        compiler_params=_params("parallel"),
        name="conv_mixers",
    )(proj, proj, proj, proj, proj, proj, proj, proj, proj, proj, proj, proj, proj,
      dw_w, dw_b2, cn2, short_w)


def _route(logits, bias):
    t, e = logits.shape
    gsize = e // N_GROUPS
    neg = -jnp.inf
    scores = jax.nn.sigmoid(logits)
    biased = scores + bias
    lane = lax.broadcasted_iota(jnp.int32, (t, e), 1)
    big = jnp.int32(e)

    def rmax(x):
        return jnp.max(x, axis=-1, keepdims=True)

    def first_lane(mask):
        return jnp.min(jnp.where(mask, lane, big), axis=-1, keepdims=True)

    gscore = jnp.zeros((t, e), F32)
    for g in range(N_GROUPS):
        in_g = (lane >= g * gsize) & (lane < (g + 1) * gsize)
        xg = jnp.where(in_g, biased, neg)
        m1 = rmax(xg)
        n_top = jnp.sum(jnp.where(xg == m1, 1.0, 0.0), axis=-1, keepdims=True)
        m2 = jnp.where(n_top >= 2.0, m1, rmax(jnp.where(xg < m1, xg, neg)))
        gscore = jnp.where(in_g, m1 + m2, gscore)

    cand = jnp.where(lane % gsize == 0, gscore, neg)
    allowed = jnp.zeros((t, e), jnp.bool_)
    for _ in range(TOPK_GROUPS):
        idx = first_lane(cand == rmax(cand))
        allowed = allowed | ((lane >= idx) & (lane < idx + gsize))
        cand = jnp.where(lane == idx, neg, cand)

    masked = jnp.where(allowed, biased, neg)
    chosen = jnp.zeros((t, e), jnp.bool_)
    for _ in range(TOP_K):
        idx = first_lane(masked == rmax(masked))
        pick = lane == idx
        chosen = chosen | pick
        masked = jnp.where(pick, neg, masked)

    w = jnp.where(chosen, scores, 0.0)
    return w / jnp.sum(w, axis=-1, keepdims=True) * ROUTED_SCALE


def _ffn_prep_kernel(x_ref, g_ref, sc_ref, sh_ref, rw_ref, rb_ref, h_ref, gates_ref):
    h = _modnorm_rows(x_ref[...], g_ref[...], sc_ref[...], sh_ref[...])
    h_ref[...] = h.astype(h_ref.dtype)
    logits = jnp.dot(h, rw_ref[...], precision=lax.Precision.HIGHEST, preferred_element_type=F32)
    gates_ref[...] = _route(logits, rb_ref[...])


def ffn_prep(x, g, sc1p, sh, router_w, router_b, seq, *, tm=128):
    m, d = x.shape
    e = router_w.shape[1]
    tm = min(tm, m)
    return pl.pallas_call(
        _ffn_prep_kernel,
        grid=(m // tm,),
        in_specs=[
            pl.BlockSpec((tm, d), lambda i: (i, 0)),
            pl.BlockSpec((1, d), lambda i: (0, 0)),
            pl.BlockSpec((None, 1, d), lambda i: ((i * tm) // seq, 0, 0)),
            pl.BlockSpec((None, 1, d), lambda i: ((i * tm) // seq, 0, 0)),
            pl.BlockSpec((d, e), lambda i: (0, 0)),
            pl.BlockSpec((1, e), lambda i: (0, 0)),
        ],
        out_specs=[pl.BlockSpec((tm, d), lambda i: (i, 0)), pl.BlockSpec((tm, e), lambda i: (i, 0))],
        out_shape=[jax.ShapeDtypeStruct((m, d), BF16), jax.ShapeDtypeStruct((m, e), F32)],
        compiler_params=_params("parallel"),
        name="ffn_prep",
    )(x, g, sc1p, sh, router_w, router_b.reshape(1, e))


def _moe_dense_kernel(x_ref, gates_ref, wg_ref, wu_ref, wd_ref, o_ref):
    e = pl.program_id(1)

    @pl.when(e == 0)
    def _():
        o_ref[...] = jnp.zeros_like(o_ref)

    x = x_ref[...]
    gates = gates_ref[...]
    lane = lax.broadcasted_iota(jnp.int32, gates.shape, 1)
    gcol = jnp.sum(jnp.where(lane == e, gates, 0.0), axis=-1, keepdims=True)
    hg = jnp.dot(x, wg_ref[...], preferred_element_type=F32)
    hu = jnp.dot(x, wu_ref[...], preferred_element_type=F32)
    hid = (_silu(hg) * hu * gcol).astype(BF16)
    o_ref[...] += jnp.dot(hid, wd_ref[...], preferred_element_type=F32)


def moe_dense(h, gates, w_gate, w_up, w_down, *, tm=512):
    m, d = h.shape
    n_exp, _, f = w_gate.shape
    tm = min(tm, m)
    return pl.pallas_call(
        _moe_dense_kernel,
        grid=(m // tm, n_exp),
        in_specs=[
            pl.BlockSpec((tm, d), lambda i, e: (i, 0)),
            pl.BlockSpec((tm, n_exp), lambda i, e: (i, 0)),
            pl.BlockSpec((None, d, f), lambda i, e: (e, 0, 0)),
            pl.BlockSpec((None, d, f), lambda i, e: (e, 0, 0)),
            pl.BlockSpec((None, f, d), lambda i, e: (e, 0, 0)),
        ],
        out_specs=pl.BlockSpec((tm, d), lambda i, e: (i, 0)),
        out_shape=jax.ShapeDtypeStruct((m, d), F32),
        compiler_params=_params("parallel", "arbitrary"),
        name="moe_dense",
    )(h, gates, w_gate, w_up, w_down)


def _glu_matmul_kernel(x_ref, wg_ref, wu_ref, o_ref):
    x = x_ref[...]
    hg = jnp.dot(x, wg_ref[...], preferred_element_type=F32)
    hu = jnp.dot(x, wu_ref[...], preferred_element_type=F32)
    o_ref[...] = (_silu(hg) * hu).astype(o_ref.dtype)


def glu_matmul(h, wg, wu, *, tm=512):
    m, d = h.shape
    f = wg.shape[1]
    tm = min(tm, m)
    return pl.pallas_call(
        _glu_matmul_kernel,
        grid=(m // tm,),
        in_specs=[pl.BlockSpec((tm, d), lambda i: (i, 0)),
                  pl.BlockSpec((d, f), lambda i: (0, 0)), pl.BlockSpec((d, f), lambda i: (0, 0))],
        out_specs=pl.BlockSpec((tm, f), lambda i: (i, 0)),
        out_shape=jax.ShapeDtypeStruct((m, f), BF16),
        compiler_params=_params("parallel"),
        name="shared_glu",
    )(h, wg, wu)


def _rmsnorm_kernel(x_ref, g_ref, o_ref):
    x = x_ref[...]
    ms = jnp.mean(x * x, axis=-1, keepdims=True)
    o_ref[...] = (x * lax.rsqrt(ms + NORM_EPS)) * g_ref[...]


def rms_norm(x, g, *, tm=256):
    m, d = x.shape
    tm = min(tm, m)
    return pl.pallas_call(
        _rmsnorm_kernel,
        grid=(m // tm,),
        in_specs=[pl.BlockSpec((tm, d), lambda i: (i, 0)), pl.BlockSpec((1, d), lambda i: (0, 0))],
        out_specs=pl.BlockSpec((tm, d), lambda i: (i, 0)),
        out_shape=jax.ShapeDtypeStruct((m, d), F32),
        compiler_params=_params("parallel"),
        name="final_norm",
    )(x, g.reshape(1, d))


def kernel(x, c, ctx, c_ctx, ada_w, ada_b, norm_mix, norm_ffn, ev_w_in, ev_w_out, ev_sink, ev_gate_norm, ev_spatial_w, ev_spatial_b, od_w_in, od_w_out, od_dw_w, od_dw_b, od_conv_norm, od_short_w, router_w, router_b, exp_w_gate, exp_w_up, exp_w_down, sh_w_gate, sh_w_up, sh_w_down, final_norm):
    b, s, d = x.shape
    n_ctx = ctx.shape[1]
    depth = ada_w.shape[0]
    m = b * s
    a_width = d // 2
    kv_width = KV_HEADS * HEAD_DIM
    assert a_width == KV_HEADS * Q_GROUP * HEAD_DIM and s % ATTN_BLOCK == 0 and n_ctx % 16 == 0

    c_rows = jnp.zeros((8, d), F32).at[:b].set(_silu(c)).at[b].set(_silu(c_ctx)).astype(BF16)
    mod = ada_modulation(c_rows, ada_w, ada_b)
    tables = rope_tables(s)
    xf = x.reshape(m, d)

    for i in range(depth):
        j = i // 2
        sh1, sc1, g1, sh2, sc2, g2 = [mod[i, :b, k * d:(k + 1) * d].reshape(b, 1, d) for k in range(6)]
        gm = norm_mix[i].reshape(1, d)
        if i % 2 == 0:
            w_in = ev_w_in[j].astype(BF16)
            proj = norm_matmul(xf, gm, 1.0 + sc1, sh1, w_in, s, name="even_in")
            csh1 = mod[i, b, 0:d].reshape(1, 1, d)
            csc1 = mod[i, b, d:2 * d].reshape(1, 1, d)
            kvx = norm_matmul(ctx.reshape(b * n_ctx, d), gm, 1.0 + csc1, csh1,
                              w_in[:, a_width:a_width + 2 * kv_width], b * n_ctx, tm=256, name="ctx_kv")
            qk = apply_rope(proj, tables, s, a_width + kv_width)
            attn = window_attention(qk, proj, kvx, ev_sink[j], b, s, n_ctx)
            gated = spatial_gating(proj, ev_gate_norm[j], ev_spatial_w[j], ev_spatial_b[j],
                                   a_width + 2 * kv_width, d - a_width)
            w_out = ev_w_out[j].astype(BF16)
            mix = [attn, gated]
        else:
            proj = norm_matmul(xf, gm, 1.0 + sc1, sh1, od_w_in[j].astype(BF16), s, name="odd_in")
            mix = list(conv_mixers(proj, od_dw_w[j], od_dw_b[j], od_conv_norm[j], od_short_w[j], s))
            w_out = od_w_out[j].astype(BF16)
        xf = resid_matmul(mix, [w_out[:a_width], w_out[a_width:]], xf, g1, s, name="mix_out")

        hf, gates = ffn_prep(xf, norm_ffn[i].reshape(1, d), 1.0 + sc2, sh2, router_w[i], router_b[i], s)
        routed = moe_dense(hf, gates, exp_w_gate[i].astype(BF16), exp_w_up[i].astype(BF16),
                           exp_w_down[i].astype(BF16))
        shared = glu_matmul(hf, sh_w_gate[i].astype(BF16), sh_w_up[i].astype(BF16))
        xf = resid_matmul([shared], [sh_w_down[i].astype(BF16)], xf, g2, s, extra=routed, name="ffn_out")

    return rms_norm(xf, final_norm).reshape(b, s, d)
```

```python
import functools

import jax
import jax.numpy as jnp
from jax import lax
from jax.experimental import pallas as pl
from jax.experimental.pallas import tpu as pltpu

F32 = jnp.float32
BF16 = jnp.bfloat16

HEAD_DIM = 128
ATTN_BLOCK = 128
WINDOW = 128
CHUNK = 128
KV_HEADS = 4
Q_GROUP = 4
CONV_WIDTH = 31
SHORT_WIDTH = 3
CONV_HALO = 16
N_EXPERTS = 64
N_GROUPS = 8
TOPK_GROUPS = 4
TOP_K = 6
ROUTED_SCALE = 2.5
ROPE_THETA = 10000.0
NORM_EPS = 1e-6
GRID_W = 64
VMEM_LIMIT = 56 * 1024 * 1024


def _params(*sem):
    return pltpu.CompilerParams(dimension_semantics=sem, vmem_limit_bytes=VMEM_LIMIT)


def _silu(x):
    return x * jax.nn.sigmoid(x)


def _ada_kernel(c_ref, w_ref, b_ref, o_ref):
    acc = jnp.dot(c_ref[...], w_ref[...].astype(BF16), preferred_element_type=F32)
    o_ref[...] = acc + b_ref[...]


def ada_modulation(c_rows, ada_w, ada_b, tn=512):
    depth, d, n = ada_w.shape
    r = c_rows.shape[0]
    return pl.pallas_call(
        _ada_kernel,
        grid=(depth, n // tn),
        in_specs=[
            pl.BlockSpec((r, d), lambda l, j: (0, 0)),
            pl.BlockSpec((None, d, tn), lambda l, j: (l, 0, j)),
            pl.BlockSpec((None, 1, tn), lambda l, j: (l, 0, j)),
        ],
        out_specs=pl.BlockSpec((None, r, tn), lambda l, j: (l, 0, j)),
        out_shape=jax.ShapeDtypeStruct((depth, r, n), F32),
        compiler_params=_params("parallel", "parallel"),
        name="ada_modulation",
    )(c_rows, ada_w, ada_b.reshape(depth, 1, n))


def _modnorm_rows(x, g, sc1p, sh):
    ms = jnp.mean(x * x, axis=-1, keepdims=True)
    return (x * lax.rsqrt(ms + NORM_EPS)) * g * sc1p + sh


def _norm_matmul_kernel(x_ref, g_ref, sc_ref, sh_ref, w_ref, o_ref, h_ref, *, rows):
    @pl.when(pl.program_id(1) == 0)
    def _():
        def body(r, carry):
            rs = pl.ds(pl.multiple_of(r * rows, rows), rows)
            h_ref[rs, :] = _modnorm_rows(x_ref[rs, :], g_ref[...], sc_ref[...], sh_ref[...]).astype(BF16)
            return carry
        lax.fori_loop(0, x_ref.shape[0] // rows, body, 0)

    o_ref[...] = jnp.dot(h_ref[...], w_ref[...], preferred_element_type=F32).astype(o_ref.dtype)


def norm_matmul(x, g, sc1p, sh, w, seq, *, tm=512, tn=512, out_dtype=BF16, name="norm_matmul"):
    m, d = x.shape
    n = w.shape[1]
    tm = min(tm, m)
    tn = min(tn, n)
    assert m % tm == 0 and n % tn == 0 and seq % tm == 0
    rows = min(128, tm)
    return pl.pallas_call(
        functools.partial(_norm_matmul_kernel, rows=rows),
        grid=(m // tm, n // tn),
        in_specs=[
            pl.BlockSpec((tm, d), lambda i, j: (i, 0)),
            pl.BlockSpec((1, d), lambda i, j: (0, 0)),
            pl.BlockSpec((None, 1, d), lambda i, j: ((i * tm) // seq, 0, 0)),
            pl.BlockSpec((None, 1, d), lambda i, j: ((i * tm) // seq, 0, 0)),
            pl.BlockSpec((d, tn), lambda i, j: (0, j)),
        ],
        out_specs=pl.BlockSpec((tm, tn), lambda i, j: (i, j)),
        out_shape=jax.ShapeDtypeStruct((m, n), out_dtype),
        scratch_shapes=[pltpu.VMEM((tm, d), BF16)],
        compiler_params=_params("parallel", "arbitrary"),
        name=name,
    )(x, g, sc1p, sh, w)


def _rope_kernel(x_ref, cos_ref, sa_ref, sb_ref, o_ref, *, heads):
    cos, sa, sb = cos_ref[...], sa_ref[...], sb_ref[...]
    for h in range(heads):
        cs = slice(h * HEAD_DIM, (h + 1) * HEAD_DIM)
        x = x_ref[:, cs].astype(F32)
        y = x * cos + pltpu.roll(x, 3 * HEAD_DIM // 4, 1) * sa + pltpu.roll(x, HEAD_DIM // 4, 1) * sb
        o_ref[:, cs] = y.astype(o_ref.dtype)


def rope_tables(seq):
    rows = seq // GRID_W
    row = jnp.repeat(jnp.arange(rows), GRID_W)
    col = jnp.tile(jnp.arange(GRID_W), rows)
    pos = jnp.stack([row, col], axis=-1).astype(F32)
    n_freq = HEAD_DIM // 4
    inv_freq = ROPE_THETA ** (-jnp.arange(n_freq, dtype=F32) / n_freq)
    ang = pos[:, :, None] * inv_freq
    ang = jnp.broadcast_to(ang[:, :, None, :], (seq, 2, 2, n_freq)).reshape(seq, HEAD_DIM)
    cos, sin = jnp.cos(ang), jnp.sin(ang)
    first = (jnp.arange(HEAD_DIM) // n_freq) % 2 == 0
    return cos, jnp.where(first, -sin, 0.0), jnp.where(first, 0.0, sin)


def apply_rope(proj, tables, seq, width, tm=128):
    m = proj.shape[0]
    nblk = seq // tm
    tspec = pl.BlockSpec((tm, HEAD_DIM), lambda i: (i % nblk, 0))
    return pl.pallas_call(
        functools.partial(_rope_kernel, heads=width // HEAD_DIM),
        grid=(m // tm,),
        in_specs=[pl.BlockSpec((tm, width), lambda i: (i, 0)), tspec, tspec, tspec],
        out_specs=pl.BlockSpec((tm, width), lambda i: (i, 0)),
        out_shape=jax.ShapeDtypeStruct((m, width), proj.dtype),
        compiler_params=_params("parallel"),
        name="rope",
    )(proj, *tables)


def _attn_kernel(sink_ref, q_ref, kp_ref, kc_ref, kn_ref, vp_ref, vc_ref, vn_ref, kx_ref, vx_ref, o_ref, *, nb):
    n = pl.program_id(1)
    kv = pl.program_id(2)
    blk = ATTN_BLOCK
    q = q_ref[...]
    qs = jnp.concatenate([q[:, g * HEAD_DIM:(g + 1) * HEAD_DIM] for g in range(Q_GROUP)], axis=0)
    keys = jnp.concatenate([kp_ref[...], kc_ref[...], kn_ref[...], kx_ref[...]], axis=0)
    vals = jnp.concatenate([vp_ref[...], vc_ref[...], vn_ref[...], vx_ref[...]], axis=0)
    s = lax.dot_general(qs, keys, (((1,), (1,)), ((), ())), preferred_element_type=F32) * (HEAD_DIM ** -0.5)
    rows, cols = s.shape
    qi = lax.broadcasted_iota(jnp.int32, (rows, cols), 0) % blk
    si = lax.broadcasted_iota(jnp.int32, (rows, cols), 1)
    rel = si - blk - qi
    kpos = n * blk + si - blk
    local_ok = (jnp.abs(rel) <= WINDOW) & (kpos >= 0) & (kpos < nb * blk)
    s = jnp.where((si >= 3 * blk) | local_ok, s, -jnp.inf)
    sink = jnp.concatenate(
        [jnp.full((blk, 1), sink_ref[kv * Q_GROUP + g], F32) for g in range(Q_GROUP)], axis=0)
    mx = jnp.maximum(jnp.max(s, axis=-1, keepdims=True), sink)
    p = jnp.exp(s - mx)
    denom = jnp.sum(p, axis=-1, keepdims=True) + jnp.exp(sink - mx)
    o = jnp.dot(p.astype(vals.dtype), vals, preferred_element_type=F32) / denom
    for g in range(Q_GROUP):
        o_ref[:, g * HEAD_DIM:(g + 1) * HEAD_DIM] = o[g * blk:(g + 1) * blk, :].astype(o_ref.dtype)


def window_attention(qk, proj, kvx, sink, batch, seq, n_ctx):
    nb = seq // ATTN_BLOCK
    m = batch * seq
    q_cols = Q_GROUP * HEAD_DIM
    k_off = KV_HEADS * Q_GROUP
    v_off = k_off + KV_HEADS

    def rows(shift):
        return lambda b, n, k, s_: b * nb + jnp.clip(n + shift, 0, nb - 1)

    def kspec(shift, off):
        r = rows(shift)
        return pl.BlockSpec((ATTN_BLOCK, HEAD_DIM), lambda b, n, k, s_: (r(b, n, k, s_), off + k))

    grid_spec = pltpu.PrefetchScalarGridSpec(
        num_scalar_prefetch=1,
        grid=(batch, nb, KV_HEADS),
        in_specs=[
            pl.BlockSpec((ATTN_BLOCK, q_cols), lambda b, n, k, s_: (b * nb + n, k)),
            kspec(-1, k_off), kspec(0, k_off), kspec(1, k_off),
            kspec(-1, v_off), kspec(0, v_off), kspec(1, v_off),
            pl.BlockSpec((n_ctx, HEAD_DIM), lambda b, n, k, s_: (b, k)),
            pl.BlockSpec((n_ctx, HEAD_DIM), lambda b, n, k, s_: (b, KV_HEADS + k)),
        ],
        out_specs=pl.BlockSpec((ATTN_BLOCK, q_cols), lambda b, n, k, s_: (b * nb + n, k)),
    )
    return pl.pallas_call(
        functools.partial(_attn_kernel, nb=nb),
        grid_spec=grid_spec,
        out_shape=jax.ShapeDtypeStruct((m, KV_HEADS * q_cols), BF16),
        compiler_params=_params("parallel", "parallel", "parallel"),
        name="window_attention",
    )(sink, qk, qk, qk, qk, proj, proj, proj, kvx, kvx)


def _gating_kernel(u0_ref, u1_ref, v0_ref, v1_ref, gn_ref, w_ref, b_ref, o_ref, *, groups):
    half = groups // 2
    v = jnp.concatenate([v0_ref[...], v1_ref[...]], axis=-1).astype(F32)
    v = jax.nn.gelu(v)
    ms = jnp.mean(v * v, axis=-1, keepdims=True)
    vn = ((v * lax.rsqrt(ms + NORM_EPS)) * gn_ref[...]).astype(BF16)
    for g in range(groups):
        cs = slice(g * HEAD_DIM, (g + 1) * HEAD_DIM)
        mixed = jnp.dot(w_ref[g], vn[:, cs], preferred_element_type=F32) + b_ref[:, g:g + 1]
        u_ref = u0_ref if g < half else u1_ref
        us = slice((g % half) * HEAD_DIM, (g % half + 1) * HEAD_DIM)
        o_ref[:, cs] = (jax.nn.gelu(u_ref[:, us].astype(F32)) * mixed).astype(o_ref.dtype)


def spatial_gating(proj, gate_norm, spatial_w, spatial_b, u_col, width):
    m = proj.shape[0]
    groups = width // HEAD_DIM
    hw = width // 2
    base = u_col // hw
    assert u_col % hw == 0

    def col(k):
        return pl.BlockSpec((CHUNK, hw), lambda i: (i, base + k))

    return pl.pallas_call(
        functools.partial(_gating_kernel, groups=groups),
        grid=(m // CHUNK,),
        in_specs=[
            col(0), col(1), col(2), col(3),
            pl.BlockSpec((1, width), lambda i: (0, 0)),
            pl.BlockSpec((groups, CHUNK, CHUNK), lambda i: (0, 0, 0)),
            pl.BlockSpec((CHUNK, groups), lambda i: (0, 0)),
        ],
        out_specs=pl.BlockSpec((CHUNK, width), lambda i: (i, 0)),
        out_shape=jax.ShapeDtypeStruct((m, width), BF16),
        compiler_params=_params("parallel"),
        name="spatial_gating",
    )(proj, proj, proj, proj, gate_norm.reshape(1, width), spatial_w.astype(BF16), spatial_b.T)


def _resid_matmul_kernel(*refs, n_pairs, has_extra):
    a_refs = refs[:n_pairs]
    w_refs = refs[n_pairs:2 * n_pairs]
    pos = 2 * n_pairs
    extra_ref = refs[pos] if has_extra else None
    pos += int(has_extra)
    res_ref, gate_ref, o_ref = refs[pos], refs[pos + 1], refs[pos + 2]
    acc = jnp.dot(a_refs[0][...], w_refs[0][...], preferred_element_type=F32)
    for a_ref, w_ref in zip(a_refs[1:], w_refs[1:]):
        acc += jnp.dot(a_ref[...], w_ref[...], preferred_element_type=F32)
    if has_extra:
        acc += extra_ref[...].astype(F32)
    o_ref[...] = res_ref[...] + gate_ref[...] * acc


def resid_matmul(a_list, w_list, res, gate, seq, extra=None, *, tm=512, tn=512, name="resid_matmul"):
    m, n = res.shape
    tm = min(tm, m)
    assert m % tm == 0 and n % tn == 0 and seq % tm == 0
    in_specs = [pl.BlockSpec((tm, a.shape[1]), lambda i, j: (i, 0)) for a in a_list]
    in_specs += [pl.BlockSpec((w.shape[0], tn), lambda i, j: (0, j)) for w in w_list]
    args = list(a_list) + list(w_list)
    if extra is not None:
        in_specs.append(pl.BlockSpec((tm, tn), lambda i, j: (i, j)))
        args.append(extra)
    in_specs += [
        pl.BlockSpec((tm, tn), lambda i, j: (i, j)),
        pl.BlockSpec((None, 1, tn), lambda i, j: ((i * tm) // seq, 0, j)),
    ]
    args += [res, gate]
    return pl.pallas_call(
        functools.partial(_resid_matmul_kernel, n_pairs=len(a_list), has_extra=extra is not None),
        grid=(m // tm, n // tn),
        in_specs=in_specs,
        out_specs=pl.BlockSpec((tm, tn), lambda i, j: (i, j)),
        out_shape=jax.ShapeDtypeStruct((m, n), F32),
        compiler_params=_params("parallel", "parallel"),
        name=name,
    )(*args)


def _conv_kernel(a_ref, ap_ref, an_ref, g_ref, gp_ref, gn_ref, gb_ref,
                 c_ref, cp_ref, cn_ref, x_ref, xp_ref, xn_ref,
                 dw_ref, db_ref, cg_ref, sw_ref, yc_ref, yd_ref, z_ref, y_ref, ssq_ref, *, tiles_per_seq):
    tp, width = a_ref.shape
    halo = CONV_HALO
    t = pl.program_id(0) % tiles_per_seq
    keep_prev = (t > 0).astype(F32)
    keep_next = (t < tiles_per_seq - 1).astype(F32)
    lane_chunks = width // 128

    def glu(a, g):
        return a.astype(F32) * jax.nn.sigmoid(g.astype(F32))

    def prod(a, b):
        return a.astype(F32) * b.astype(F32)

    def fill(main, prev, nxt):
        z_ref[pl.ds(0, halo), :] = prev * keep_prev
        z_ref[pl.ds(halo, tp), :] = main
        z_ref[pl.ds(halo + tp, halo), :] = nxt * keep_next

    def conv(w_ref, taps, cs):
        pad = taps // 2
        acc = jnp.zeros((tp, 128), F32)
        for k in range(taps):
            acc = acc + z_ref[pl.ds(halo + k - pad, tp), cs] * w_ref[pl.ds(k, 1), cs]
        return acc

    fill(glu(a_ref[...], g_ref[...]), glu(ap_ref[...], gp_ref[...]), glu(an_ref[...], gn_ref[...]))
    ssq_ref[...] = jnp.zeros_like(ssq_ref)

    def conv_c(c, carry):
        cs = pl.ds(pl.multiple_of(c * 128, 128), 128)
        y = conv(dw_ref, CONV_WIDTH, cs) + db_ref[:, cs]
        y_ref[:, cs] = y
        ssq_ref[...] += y * y
        return carry
    lax.fori_loop(0, lane_chunks, conv_c, 0)
    inv = lax.rsqrt(jnp.sum(ssq_ref[...], axis=-1, keepdims=True) / width + NORM_EPS)

    def norm_c(c, carry):
        cs = pl.ds(pl.multiple_of(c * 128, 128), 128)
        yc_ref[:, cs] = _silu(y_ref[:, cs] * inv * cg_ref[:, cs]).astype(yc_ref.dtype)
        return carry
    lax.fori_loop(0, lane_chunks, norm_c, 0)

    fill(prod(c_ref[...], x_ref[...]), prod(cp_ref[...], xp_ref[...]), prod(cn_ref[...], xn_ref[...]))

    def conv_d(c, carry):
        cs = pl.ds(pl.multiple_of(c * 128, 128), 128)
        yd_ref[:, cs] = (gb_ref[:, cs].astype(F32) * conv(sw_ref, SHORT_WIDTH, cs)).astype(yd_ref.dtype)
        return carry
    lax.fori_loop(0, lane_chunks, conv_d, 0)


def conv_mixers(proj, dw_w, dw_b, conv_norm, short_w, seq, *, tp=256):
    m = proj.shape[0]
    width = proj.shape[1] // 5
    tp = min(tp, seq)
    assert seq % tp == 0 and tp % CONV_HALO == 0
    per = tp // CONV_HALO
    n_halo = m // CONV_HALO

    def main(cb):
        return pl.BlockSpec((tp, width), lambda i: (i, cb))

    def prev(cb):
        return pl.BlockSpec((CONV_HALO, width), lambda i: (jnp.maximum(i * per - 1, 0), cb))

    def nxt(cb):
        return pl.BlockSpec((CONV_HALO, width), lambda i: (jnp.minimum((i + 1) * per, n_halo - 1), cb))

    def full(arr):
        return pl.BlockSpec(arr.shape, lambda i: (0, 0))

    dw_b2, cn2 = dw_b.reshape(1, width), conv_norm.reshape(1, width)
    out = jax.ShapeDtypeStruct((m, width), BF16)
    return pl.pallas_call(
        functools.partial(_conv_kernel, tiles_per_seq=seq // tp),
        grid=(m // tp,),
        in_specs=[main(0), prev(0), nxt(0), main(1), prev(1), nxt(1), main(2),
                  main(3), prev(3), nxt(3), main(4), prev(4), nxt(4),
                  full(dw_w), full(dw_b2), full(cn2), full(short_w)],
        out_specs=[pl.BlockSpec((tp, width), lambda i: (i, 0))] * 2,
        out_shape=[out, out],
        scratch_shapes=[pltpu.VMEM((tp + 2 * CONV_HALO, width), F32), pltpu.VMEM((tp, width), F32),
                        pltpu.VMEM((tp, 128), F32)],
        compiler_params=_params("parallel"),
        name="conv_mixers",
    )(proj, proj, proj, proj, proj, proj, proj, proj, proj, proj, proj, proj, proj,
      dw_w, dw_b2, cn2, short_w)


def _route(logits, bias):
    t, e = logits.shape
    gsize = e // N_GROUPS
    neg = -jnp.inf
    scores = jax.nn.sigmoid(logits)
    biased = scores + bias
    lane = lax.broadcasted_iota(jnp.int32, (t, e), 1)
    big = jnp.int32(e)

    def rmax(x):
        return jnp.max(x, axis=-1, keepdims=True)

    def first_lane(mask):
        return jnp.min(jnp.where(mask, lane, big), axis=-1, keepdims=True)

    gscore = jnp.zeros((t, e), F32)
    for g in range(N_GROUPS):
        in_g = (lane >= g * gsize) & (lane < (g + 1) * gsize)
        xg = jnp.where(in_g, biased, neg)
        m1 = rmax(xg)
        n_top = jnp.sum(jnp.where(xg == m1, 1.0, 0.0), axis=-1, keepdims=True)
        m2 = jnp.where(n_top >= 2.0, m1, rmax(jnp.where(xg < m1, xg, neg)))
        gscore = jnp.where(in_g, m1 + m2, gscore)

    cand = jnp.where(lane % gsize == 0, gscore, neg)
    allowed = jnp.zeros((t, e), jnp.bool_)
    for _ in range(TOPK_GROUPS):
        idx = first_lane(cand == rmax(cand))
        allowed = allowed | ((lane >= idx) & (lane < idx + gsize))
        cand = jnp.where(lane == idx, neg, cand)

    masked = jnp.where(allowed, biased, neg)
    chosen = jnp.zeros((t, e), jnp.bool_)
    for _ in range(TOP_K):
        idx = first_lane(masked == rmax(masked))
        pick = lane == idx
        chosen = chosen | pick
        masked = jnp.where(pick, neg, masked)

    w = jnp.where(chosen, scores, 0.0)
    return w / jnp.sum(w, axis=-1, keepdims=True) * ROUTED_SCALE


def _ffn_prep_kernel(x_ref, g_ref, sc_ref, sh_ref, rw_ref, rb_ref, h_ref, gates_ref):
    h = _modnorm_rows(x_ref[...], g_ref[...], sc_ref[...], sh_ref[...])
    h_ref[...] = h.astype(h_ref.dtype)
    logits = jnp.dot(h, rw_ref[...], precision=lax.Precision.HIGHEST, preferred_element_type=F32)
    gates_ref[...] = _route(logits, rb_ref[...])


def ffn_prep(x, g, sc1p, sh, router_w, router_b, seq, *, tm=128):
    m, d = x.shape
    e = router_w.shape[1]
    tm = min(tm, m)
    return pl.pallas_call(
        _ffn_prep_kernel,
        grid=(m // tm,),
        in_specs=[
            pl.BlockSpec((tm, d), lambda i: (i, 0)),
            pl.BlockSpec((1, d), lambda i: (0, 0)),
            pl.BlockSpec((None, 1, d), lambda i: ((i * tm) // seq, 0, 0)),
            pl.BlockSpec((None, 1, d), lambda i: ((i * tm) // seq, 0, 0)),
            pl.BlockSpec((d, e), lambda i: (0, 0)),
            pl.BlockSpec((1, e), lambda i: (0, 0)),
        ],
        out_specs=[pl.BlockSpec((tm, d), lambda i: (i, 0)), pl.BlockSpec((tm, e), lambda i: (i, 0))],
        out_shape=[jax.ShapeDtypeStruct((m, d), BF16), jax.ShapeDtypeStruct((m, e), F32)],
        compiler_params=_params("parallel"),
        name="ffn_prep",
    )(x, g, sc1p, sh, router_w, router_b.reshape(1, e))


MOE_TB = 256
MOE_UNIT = 16
MOE_CHUNK = 256
MOE_TM = 256
UNITS_PER_CHUNK = MOE_CHUNK // MOE_UNIT


def _ceil_to(x, k):
    return (x + k - 1) // k * k


def _moe_sizes(m, e, tb):
    nb = m // tb
    pad_units = -(-(MOE_TM // MOE_UNIT - 1) // nb)
    rows_loc = _ceil_to(tb * TOP_K + e * (MOE_UNIT - 1) + e * MOE_UNIT * pad_units, MOE_CHUNK)
    rows_max = _ceil_to(m * TOP_K + nb * e * (MOE_UNIT - 1) + e * (MOE_TM - MOE_UNIT), MOE_TM)
    return rows_loc // MOE_UNIT, rows_max


def _moe_plan(gates, tb):
    m, e = gates.shape
    nb = m // tb
    umax, rows_max = _moe_sizes(m, e, tb)
    sel = (gates > 0).reshape(nb, tb, e)
    seli = sel.astype(jnp.int32)
    rank = jnp.where(sel, jnp.cumsum(seli, axis=1) - seli, -1)
    seg = _ceil_to(jnp.sum(seli, axis=1), MOE_UNIT)
    tot = jnp.sum(seg, axis=0)
    extra = (_ceil_to(tot, MOE_TM) - tot) // MOE_UNIT
    spread = extra[None, :] // nb + (jnp.arange(nb)[:, None] < extra[None, :] % nb)
    seg = seg + MOE_UNIT * spread
    loff = jnp.cumsum(seg, axis=1) - seg
    lend = loff + seg
    tot = jnp.sum(seg, axis=0)
    gstart = jnp.cumsum(tot) - tot
    goff = gstart[None, :] + jnp.cumsum(seg, axis=0) - seg
    n_units = jnp.sum(seg, axis=1) // MOE_UNIT

    urow = jnp.arange(umax) * MOE_UNIT
    inseg = (loff[:, None, :] <= urow[None, :, None]) & (urow[None, :, None] < lend[:, None, :])
    dst = (jnp.sum(jnp.where(inseg, (goff - loff)[:, None, :], 0), axis=-1) + urow[None, :]) // MOE_UNIT
    dst = jnp.where(jnp.any(inseg, axis=-1), dst, 0).astype(jnp.int32)

    tmax = rows_max // MOE_TM
    n_tiles = jnp.sum(tot) // MOE_TM
    tile = jnp.minimum(jnp.arange(tmax), n_tiles - 1)
    tile_exp = jnp.sum((gstart + tot)[None, :] <= (tile * MOE_TM)[:, None], axis=-1)
    return dict(
        rank=rank.astype(BF16), rank_t=jnp.swapaxes(rank, 1, 2).astype(BF16),
        loff_row=loff.astype(F32).reshape(nb, 1, e), lend_row=lend.astype(F32).reshape(nb, 1, e),
        loff_col=loff.astype(F32).reshape(nb, e, 1), lend_col=lend.astype(F32).reshape(nb, e, 1),
        dst=dst.reshape(-1), n_units=n_units.astype(jnp.int32), umax=umax,
        tile=tile.astype(jnp.int32), tile_exp=tile_exp.astype(jnp.int32),
        n_tiles=n_tiles.astype(jnp.int32).reshape(1))


def _unit_copies(dst_ref, hbm_ref, buf_ref, sem_ref, *, umax, to_hbm):
    base = pl.program_id(0) * umax

    def copy(c, u):
        g = dst_ref[base + c * UNITS_PER_CHUNK + u]
        slot = c % 2
        loc = buf_ref.at[slot, pl.ds(u * MOE_UNIT, MOE_UNIT), :]
        glob = hbm_ref.at[pl.ds(pl.multiple_of(g * MOE_UNIT, MOE_UNIT), MOE_UNIT), :]
        src, dst = (loc, glob) if to_hbm else (glob, loc)
        return pltpu.make_async_copy(src, dst, sem_ref.at[slot])
    return copy


def _for_units(c, n_units, fn):
    for u in range(UNITS_PER_CHUNK):
        @pl.when(c * UNITS_PER_CHUNK + u < n_units)
        def _():
            fn(u)


def _dispatch_kernel(dst_ref, nun_ref, h_ref, rank_t_ref, loff_ref, lend_ref, rows_in_ref, xs_ref, buf_ref, sem_ref,
                     *, umax):
    del rows_in_ref
    n_units = nun_ref[pl.program_id(0)]
    n_chunks = (n_units + UNITS_PER_CHUNK - 1) // UNITS_PER_CHUNK
    copy = _unit_copies(dst_ref, xs_ref, buf_ref, sem_ref, umax=umax, to_hbm=True)
    n_exp = loff_ref.shape[-1]

    def chunk(c, carry):
        @pl.when(c >= 2)
        def _():
            _for_units(c - 2, n_units, lambda u: copy(c - 2, u).wait())
        r = (c * MOE_CHUNK + lax.broadcasted_iota(jnp.int32, (MOE_CHUNK, n_exp), 0)).astype(F32)
        loff = loff_ref[...]
        in_seg = (loff <= r) & (r < lend_ref[...])
        j = r[:, :1] - jnp.sum(jnp.where(in_seg, loff, 0.0), axis=-1, keepdims=True)
        rk = jnp.dot(jnp.where(in_seg, 1.0, 0.0).astype(BF16), rank_t_ref[...], preferred_element_type=F32)
        one_hot = jnp.where(rk == j, 1.0, 0.0).astype(BF16)
        buf_ref[c % 2] = jnp.dot(one_hot, h_ref[...], preferred_element_type=F32).astype(buf_ref.dtype)
        _for_units(c, n_units, lambda u: copy(c, u).start())
        return carry
    lax.fori_loop(0, n_chunks, chunk, 0)

    @pl.when(n_chunks >= 2)
    def _():
        _for_units(n_chunks - 2, n_units, lambda u: copy(n_chunks - 2, u).wait())
    _for_units(n_chunks - 1, n_units, lambda u: copy(n_chunks - 1, u).wait())


def moe_dispatch(h, plan, tb, rows_buf):
    m, d = h.shape
    e = plan["rank"].shape[-1]
    umax = plan["umax"]
    grid_spec = pltpu.PrefetchScalarGridSpec(
        num_scalar_prefetch=2,
        grid=(m // tb,),
        in_specs=[
            pl.BlockSpec((tb, d), lambda b, *_: (b, 0)),
            pl.BlockSpec((None, e, tb), lambda b, *_: (b, 0, 0)),
            pl.BlockSpec((None, 1, e), lambda b, *_: (b, 0, 0)),
            pl.BlockSpec((None, 1, e), lambda b, *_: (b, 0, 0)),
            pl.BlockSpec(memory_space=pl.ANY),
        ],
        out_specs=pl.BlockSpec(memory_space=pl.ANY),
        scratch_shapes=[pltpu.VMEM((2, MOE_CHUNK, d), h.dtype), pltpu.SemaphoreType.DMA((2,))],
    )
    return pl.pallas_call(
        functools.partial(_dispatch_kernel, umax=umax),
        grid_spec=grid_spec,
        out_shape=jax.ShapeDtypeStruct(rows_buf.shape, rows_buf.dtype),
        input_output_aliases={6: 0},
        compiler_params=_params("arbitrary"),
        name="moe_dispatch",
    )(plan["dst"], plan["n_units"], h, plan["rank_t"], plan["loff_row"], plan["lend_row"], rows_buf)


def _experts_kernel(tile_ref, exp_ref, nt_ref, x_ref, wg_ref, wu_ref, wd_ref, o_ref):
    @pl.when(pl.program_id(0) < nt_ref[0])
    def _():
        x = x_ref[...]
        hg = jnp.dot(x, wg_ref[...], preferred_element_type=F32)
        hu = jnp.dot(x, wu_ref[...], preferred_element_type=F32)
        hid = (_silu(hg) * hu).astype(BF16)
        o_ref[...] = jnp.dot(hid, wd_ref[...], preferred_element_type=F32).astype(o_ref.dtype)


def moe_experts(xs, plan, w_gate, w_up, w_down):
    rows, d = xs.shape
    f = w_gate.shape[-1]
    grid_spec = pltpu.PrefetchScalarGridSpec(
        num_scalar_prefetch=3,
        grid=(rows // MOE_TM,),
        in_specs=[
            pl.BlockSpec((MOE_TM, d), lambda t, tile, ex, nt: (tile[t], 0)),
            pl.BlockSpec((None, d, f), lambda t, tile, ex, nt: (ex[t], 0, 0)),
            pl.BlockSpec((None, d, f), lambda t, tile, ex, nt: (ex[t], 0, 0)),
            pl.BlockSpec((None, f, d), lambda t, tile, ex, nt: (ex[t], 0, 0)),
        ],
        out_specs=pl.BlockSpec((MOE_TM, d), lambda t, tile, ex, nt: (tile[t], 0)),
    )
    return pl.pallas_call(
        _experts_kernel,
        grid_spec=grid_spec,
        out_shape=jax.ShapeDtypeStruct((rows, d), xs.dtype),
        input_output_aliases={3: 0},
        compiler_params=_params("arbitrary"),
        name="moe_experts",
    )(plan["tile"], plan["tile_exp"], plan["n_tiles"], xs, w_gate, w_up, w_down)


def _combine_kernel(dst_ref, nun_ref, rank_ref, gates_ref, loff_ref, lend_ref, y_ref, o_ref, buf_ref, sem_ref,
                    *, umax):
    n_units = nun_ref[pl.program_id(0)]
    n_chunks = (n_units + UNITS_PER_CHUNK - 1) // UNITS_PER_CHUNK
    copy = _unit_copies(dst_ref, y_ref, buf_ref, sem_ref, umax=umax, to_hbm=False)
    n_exp = loff_ref.shape[0]

    @pl.when(pl.program_id(0) == 0)
    def _():
        buf_ref[...] = jnp.zeros_like(buf_ref)

    _for_units(0, n_units, lambda u: copy(0, u).start())
    o_ref[...] = jnp.zeros_like(o_ref)
    rank = rank_ref[...]
    gates = gates_ref[...].astype(BF16)

    def chunk(c, carry):
        @pl.when(c + 1 < n_chunks)
        def _():
            _for_units(c + 1, n_units, lambda u: copy(c + 1, u).start())
        _for_units(c, n_units, lambda u: copy(c, u).wait())
        r = (c * MOE_CHUNK + lax.broadcasted_iota(jnp.int32, (n_exp, MOE_CHUNK), 1)).astype(F32)
        loff = loff_ref[...]
        in_seg = (loff <= r) & (r < lend_ref[...])
        j = r[:1, :] - jnp.sum(jnp.where(in_seg, loff, 0.0), axis=0, keepdims=True)
        seg_t = jnp.where(in_seg, 1.0, 0.0).astype(BF16)
        rk = jnp.dot(rank, seg_t, preferred_element_type=F32)
        gt = jnp.dot(gates, seg_t, preferred_element_type=F32)
        weights = jnp.where(rk == j, gt, 0.0).astype(BF16)
        o_ref[...] += jnp.dot(weights, buf_ref[c % 2], preferred_element_type=F32)
        return carry
    lax.fori_loop(0, n_chunks, chunk, 0)


def moe_combine(y, gates, plan, tb):
    m, e = gates.shape
    d = y.shape[1]
    umax = plan["umax"]
    grid_spec = pltpu.PrefetchScalarGridSpec(
        num_scalar_prefetch=2,
        grid=(m // tb,),
        in_specs=[
            pl.BlockSpec((None, tb, e), lambda b, *_: (b, 0, 0)),
            pl.BlockSpec((tb, e), lambda b, *_: (b, 0)),
            pl.BlockSpec((None, e, 1), lambda b, *_: (b, 0, 0)),
            pl.BlockSpec((None, e, 1), lambda b, *_: (b, 0, 0)),
            pl.BlockSpec(memory_space=pl.ANY),
        ],
        out_specs=pl.BlockSpec((tb, d), lambda b, *_: (b, 0)),
        scratch_shapes=[pltpu.VMEM((2, MOE_CHUNK, d), y.dtype), pltpu.SemaphoreType.DMA((2,))],
    )
    return pl.pallas_call(
        functools.partial(_combine_kernel, umax=umax),
        grid_spec=grid_spec,
        out_shape=jax.ShapeDtypeStruct((m, d), F32),
        compiler_params=_params("arbitrary"),
        name="moe_combine",
    )(plan["dst"], plan["n_units"], plan["rank"], gates, plan["loff_col"], plan["lend_col"], y)


def moe_routed(h, gates, w_gate, w_up, w_down, rows_buf):
    tb = min(MOE_TB, h.shape[0])
    plan = _moe_plan(gates, tb)
    xs = moe_dispatch(h, plan, tb, rows_buf)
    y = moe_experts(xs, plan, w_gate, w_up, w_down)
    return moe_combine(y, gates, plan, tb), y


def moe_rows_buffer(m, d, n_exp):
    return jnp.zeros((_moe_sizes(m, n_exp, min(MOE_TB, m))[1], d), BF16)


def _glu_matmul_kernel(x_ref, wg_ref, wu_ref, o_ref):
    x = x_ref[...]
    hg = jnp.dot(x, wg_ref[...], preferred_element_type=F32)
    hu = jnp.dot(x, wu_ref[...], preferred_element_type=F32)
    o_ref[...] = (_silu(hg) * hu).astype(o_ref.dtype)


def glu_matmul(h, wg, wu, *, tm=512):
    m, d = h.shape
    f = wg.shape[1]
    tm = min(tm, m)
    return pl.pallas_call(
        _glu_matmul_kernel,
        grid=(m // tm,),
        in_specs=[pl.BlockSpec((tm, d), lambda i: (i, 0)),
                  pl.BlockSpec((d, f), lambda i: (0, 0)), pl.BlockSpec((d, f), lambda i: (0, 0))],
        out_specs=pl.BlockSpec((tm, f), lambda i: (i, 0)),
        out_shape=jax.ShapeDtypeStruct((m, f), BF16),
        compiler_params=_params("parallel"),
        name="shared_glu",
    )(h, wg, wu)


def _rmsnorm_kernel(x_ref, g_ref, o_ref):
    x = x_ref[...]
    ms = jnp.mean(x * x, axis=-1, keepdims=True)
    o_ref[...] = (x * lax.rsqrt(ms + NORM_EPS)) * g_ref[...]


def rms_norm(x, g, *, tm=256):
    m, d = x.shape
    tm = min(tm, m)
    return pl.pallas_call(
        _rmsnorm_kernel,
        grid=(m // tm,),
        in_specs=[pl.BlockSpec((tm, d), lambda i: (i, 0)), pl.BlockSpec((1, d), lambda i: (0, 0))],
        out_specs=pl.BlockSpec((tm, d), lambda i: (i, 0)),
        out_shape=jax.ShapeDtypeStruct((m, d), F32),
        compiler_params=_params("parallel"),
        name="final_norm",
    )(x, g.reshape(1, d))


def kernel(x, c, ctx, c_ctx, ada_w, ada_b, norm_mix, norm_ffn, ev_w_in, ev_w_out, ev_sink, ev_gate_norm, ev_spatial_w, ev_spatial_b, od_w_in, od_w_out, od_dw_w, od_dw_b, od_conv_norm, od_short_w, router_w, router_b, exp_w_gate, exp_w_up, exp_w_down, sh_w_gate, sh_w_up, sh_w_down, final_norm):
    b, s, d = x.shape
    n_ctx = ctx.shape[1]
    depth = ada_w.shape[0]
    m = b * s
    a_width = d // 2
    kv_width = KV_HEADS * HEAD_DIM
    assert a_width == KV_HEADS * Q_GROUP * HEAD_DIM and s % ATTN_BLOCK == 0 and n_ctx % 16 == 0

    c_rows = jnp.zeros((8, d), F32).at[:b].set(_silu(c)).at[b].set(_silu(c_ctx)).astype(BF16)
    mod = ada_modulation(c_rows, ada_w, ada_b)
    tables = rope_tables(s)
    xf = x.reshape(m, d)
    rows_buf = moe_rows_buffer(m, d, router_w.shape[-1])

    for i in range(depth):
        j = i // 2
        sh1, sc1, g1, sh2, sc2, g2 = [mod[i, :b, k * d:(k + 1) * d].reshape(b, 1, d) for k in range(6)]
        gm = norm_mix[i].reshape(1, d)
        if i % 2 == 0:
            w_in = ev_w_in[j].astype(BF16)
            proj = norm_matmul(xf, gm, 1.0 + sc1, sh1, w_in, s, name="even_in")
            csh1 = mod[i, b, 0:d].reshape(1, 1, d)
            csc1 = mod[i, b, d:2 * d].reshape(1, 1, d)
            kvx = norm_matmul(ctx.reshape(b * n_ctx, d), gm, 1.0 + csc1, csh1,
                              w_in[:, a_width:a_width + 2 * kv_width], b * n_ctx, tm=256, name="ctx_kv")
            qk = apply_rope(proj, tables, s, a_width + kv_width)
            attn = window_attention(qk, proj, kvx, ev_sink[j], b, s, n_ctx)
            gated = spatial_gating(proj, ev_gate_norm[j], ev_spatial_w[j], ev_spatial_b[j],
                                   a_width + 2 * kv_width, d - a_width)
            w_out = ev_w_out[j].astype(BF16)
            mix = [attn, gated]
        else:
            proj = norm_matmul(xf, gm, 1.0 + sc1, sh1, od_w_in[j].astype(BF16), s, name="odd_in")
            mix = list(conv_mixers(proj, od_dw_w[j], od_dw_b[j], od_conv_norm[j], od_short_w[j], s))
            w_out = od_w_out[j].astype(BF16)
        xf = resid_matmul(mix, [w_out[:a_width], w_out[a_width:]], xf, g1, s, name="mix_out")

        hf, gates = ffn_prep(xf, norm_ffn[i].reshape(1, d), 1.0 + sc2, sh2, router_w[i], router_b[i], s)
        routed, rows_buf = moe_routed(hf, gates, exp_w_gate[i].astype(BF16), exp_w_up[i].astype(BF16),
                                      exp_w_down[i].astype(BF16), rows_buf)
        shared = glu_matmul(hf, sh_w_gate[i].astype(BF16), sh_w_up[i].astype(BF16))
        xf = resid_matmul([shared], [sh_w_down[i].astype(BF16)], xf, g2, s, extra=routed, name="ffn_out")

    return rms_norm(xf, final_norm).reshape(b, s, d)
```

```python
import functools
import math

import jax
import jax.numpy as jnp
from jax import lax
from jax.experimental import pallas as pl
from jax.experimental.pallas import tpu as pltpu

F32 = jnp.float32
BF16 = jnp.bfloat16

HEAD_DIM = 128
ATTN_BLOCK = 128
WINDOW = 128
CHUNK = 128
KV_HEADS = 4
Q_GROUP = 4
CONV_WIDTH = 31
SHORT_WIDTH = 3
CONV_HALO = 16
N_EXPERTS = 64
N_GROUPS = 8
TOPK_GROUPS = 4
TOP_K = 6
ROUTED_SCALE = 2.5
ROPE_THETA = 10000.0
NORM_EPS = 1e-6
GRID_W = 64
VMEM_LIMIT = 56 * 1024 * 1024


def _params(*sem):
    return pltpu.CompilerParams(dimension_semantics=sem, vmem_limit_bytes=VMEM_LIMIT)


def _silu(x):
    return x * jax.nn.sigmoid(x)


def _ada_kernel(c_ref, w_ref, b_ref, o_ref):
    acc = jnp.dot(c_ref[...], w_ref[...].astype(BF16), preferred_element_type=F32)
    o_ref[...] = acc + b_ref[...]


def ada_modulation(c_rows, ada_w, ada_b, tn=512):
    depth, d, n = ada_w.shape
    r = c_rows.shape[0]
    return pl.pallas_call(
        _ada_kernel,
        grid=(depth, n // tn),
        in_specs=[
            pl.BlockSpec((r, d), lambda l, j: (0, 0)),
            pl.BlockSpec((None, d, tn), lambda l, j: (l, 0, j)),
            pl.BlockSpec((None, 1, tn), lambda l, j: (l, 0, j)),
        ],
        out_specs=pl.BlockSpec((None, r, tn), lambda l, j: (l, 0, j)),
        out_shape=jax.ShapeDtypeStruct((depth, r, n), F32),
        compiler_params=_params("parallel", "parallel"),
        name="ada_modulation",
    )(c_rows, ada_w, ada_b.reshape(depth, 1, n))


def _modnorm_rows(x, g, sc1p, sh):
    ms = jnp.mean(x * x, axis=-1, keepdims=True)
    return (x * lax.rsqrt(ms + NORM_EPS)) * g * sc1p + sh


def _modnorm_kernel(x_ref, g_ref, sc_ref, sh_ref, h_ref):
    h_ref[...] = _modnorm_rows(x_ref[...], g_ref[...], sc_ref[...], sh_ref[...]).astype(h_ref.dtype)


def modnorm(x, g, sc1p, sh, seq, *, tm=256):
    m, d = x.shape
    tm = min(tm, m)
    assert m % tm == 0 and seq % tm == 0
    return pl.pallas_call(
        _modnorm_kernel,
        grid=(m // tm,),
        in_specs=[
            pl.BlockSpec((tm, d), lambda i: (i, 0)),
            pl.BlockSpec((1, d), lambda i: (0, 0)),
            pl.BlockSpec((None, 1, d), lambda i: ((i * tm) // seq, 0, 0)),
            pl.BlockSpec((None, 1, d), lambda i: ((i * tm) // seq, 0, 0)),
        ],
        out_specs=pl.BlockSpec((tm, d), lambda i: (i, 0)),
        out_shape=jax.ShapeDtypeStruct((m, d), BF16),
        compiler_params=_params("parallel"),
        name="modnorm",
    )(x, g, sc1p, sh)


CAST_ROWS = 512
PROJ_TN = 512


def _cast_on_first_row_block(w_refs, wb_refs):
    @pl.when(pl.program_id(1) == 0)
    def _():
        for w_ref, wb_ref in zip(w_refs, wb_refs):
            step = math.gcd(CAST_ROWS, w_ref.shape[0])

            def body(r, carry, w_ref=w_ref, wb_ref=wb_ref, step=step):
                rs = pl.ds(pl.multiple_of(r * step, step), step)
                wb_ref[rs, :] = w_ref[rs, :].astype(wb_ref.dtype)
                return carry
            lax.fori_loop(0, w_ref.shape[0] // step, body, 0)


def _wspec(w, layer, tn, row_block=0, rows=None, col0=0):
    rows = w.shape[1] if rows is None else rows
    return pl.BlockSpec((None, rows, tn), lambda j, i: (layer, row_block, j + col0))


def _proj_kernel(x_ref, w_ref, o_ref, wb_ref):
    _cast_on_first_row_block([w_ref], [wb_ref])
    o_ref[...] = jnp.dot(x_ref[...], wb_ref[...], preferred_element_type=F32).astype(o_ref.dtype)


def proj_matmul(x, w, layer, *, n=None, col0=0, tm=1024, tn=PROJ_TN, name="proj"):
    m, k = x.shape
    n = w.shape[2] if n is None else n
    tm = min(tm, m)
    assert m % tm == 0 and n % tn == 0
    return pl.pallas_call(
        _proj_kernel,
        grid=(n // tn, m // tm),
        in_specs=[pl.BlockSpec((tm, k), lambda j, i: (i, 0)), _wspec(w, layer, tn, col0=col0)],
        out_specs=pl.BlockSpec((tm, tn), lambda j, i: (i, j)),
        out_shape=jax.ShapeDtypeStruct((m, n), BF16),
        scratch_shapes=[pltpu.VMEM((k, tn), BF16)],
        compiler_params=_params("parallel", "arbitrary"),
        name=name,
    )(x, w)


def _rope_kernel(x_ref, cos_ref, sa_ref, sb_ref, o_ref, *, heads):
    cos, sa, sb = cos_ref[...], sa_ref[...], sb_ref[...]
    for h in range(heads):
        cs = slice(h * HEAD_DIM, (h + 1) * HEAD_DIM)
        x = x_ref[:, cs].astype(F32)
        y = x * cos + pltpu.roll(x, 3 * HEAD_DIM // 4, 1) * sa + pltpu.roll(x, HEAD_DIM // 4, 1) * sb
        o_ref[:, cs] = y.astype(o_ref.dtype)


def rope_tables(seq):
    rows = seq // GRID_W
    row = jnp.repeat(jnp.arange(rows), GRID_W)
    col = jnp.tile(jnp.arange(GRID_W), rows)
    pos = jnp.stack([row, col], axis=-1).astype(F32)
    n_freq = HEAD_DIM // 4
    inv_freq = ROPE_THETA ** (-jnp.arange(n_freq, dtype=F32) / n_freq)
    ang = pos[:, :, None] * inv_freq
    ang = jnp.broadcast_to(ang[:, :, None, :], (seq, 2, 2, n_freq)).reshape(seq, HEAD_DIM)
    cos, sin = jnp.cos(ang), jnp.sin(ang)
    first = (jnp.arange(HEAD_DIM) // n_freq) % 2 == 0
    return cos, jnp.where(first, -sin, 0.0), jnp.where(first, 0.0, sin)


def apply_rope(proj, tables, seq, width, tm=128):
    m = proj.shape[0]
    nblk = seq // tm
    tspec = pl.BlockSpec((tm, HEAD_DIM), lambda i: (i % nblk, 0))
    return pl.pallas_call(
        functools.partial(_rope_kernel, heads=width // HEAD_DIM),
        grid=(m // tm,),
        in_specs=[pl.BlockSpec((tm, width), lambda i: (i, 0)), tspec, tspec, tspec],
        out_specs=pl.BlockSpec((tm, width), lambda i: (i, 0)),
        out_shape=jax.ShapeDtypeStruct((m, width), proj.dtype),
        compiler_params=_params("parallel"),
        name="rope",
    )(proj, *tables)


def _attn_kernel(sink_ref, q_ref, kp_ref, kc_ref, kn_ref, vp_ref, vc_ref, vn_ref, kx_ref, vx_ref, o_ref, *, nb):
    n = pl.program_id(1)
    kv = pl.program_id(2)
    blk = ATTN_BLOCK
    q = q_ref[...]
    qs = jnp.concatenate([q[:, g * HEAD_DIM:(g + 1) * HEAD_DIM] for g in range(Q_GROUP)], axis=0)
    keys = jnp.concatenate([kp_ref[...], kc_ref[...], kn_ref[...], kx_ref[...]], axis=0)
    vals = jnp.concatenate([vp_ref[...], vc_ref[...], vn_ref[...], vx_ref[...]], axis=0)
    s = lax.dot_general(qs, keys, (((1,), (1,)), ((), ())), preferred_element_type=F32) * (HEAD_DIM ** -0.5)
    rows, cols = s.shape
    qi = lax.broadcasted_iota(jnp.int32, (rows, cols), 0) % blk
    si = lax.broadcasted_iota(jnp.int32, (rows, cols), 1)
    rel = si - blk - qi
    kpos = n * blk + si - blk
    local_ok = (jnp.abs(rel) <= WINDOW) & (kpos >= 0) & (kpos < nb * blk)
    s = jnp.where((si >= 3 * blk) | local_ok, s, -jnp.inf)
    sink = jnp.concatenate(
        [jnp.full((blk, 1), sink_ref[kv * Q_GROUP + g], F32) for g in range(Q_GROUP)], axis=0)
    mx = jnp.maximum(jnp.max(s, axis=-1, keepdims=True), sink)
    p = jnp.exp(s - mx)
    denom = jnp.sum(p, axis=-1, keepdims=True) + jnp.exp(sink - mx)
    o = jnp.dot(p.astype(vals.dtype), vals, preferred_element_type=F32) / denom
    for g in range(Q_GROUP):
        o_ref[:, g * HEAD_DIM:(g + 1) * HEAD_DIM] = o[g * blk:(g + 1) * blk, :].astype(o_ref.dtype)


def window_attention(qk, proj, kvx, sink, batch, seq, n_ctx):
    nb = seq // ATTN_BLOCK
    m = batch * seq
    q_cols = Q_GROUP * HEAD_DIM
    k_off = KV_HEADS * Q_GROUP
    v_off = k_off + KV_HEADS

    def rows(shift):
        return lambda b, n, k, s_: b * nb + jnp.clip(n + shift, 0, nb - 1)

    def kspec(shift, off):
        r = rows(shift)
        return pl.BlockSpec((ATTN_BLOCK, HEAD_DIM), lambda b, n, k, s_: (r(b, n, k, s_), off + k))

    grid_spec = pltpu.PrefetchScalarGridSpec(
        num_scalar_prefetch=1,
        grid=(batch, nb, KV_HEADS),
        in_specs=[
            pl.BlockSpec((ATTN_BLOCK, q_cols), lambda b, n, k, s_: (b * nb + n, k)),
            kspec(-1, k_off), kspec(0, k_off), kspec(1, k_off),
            kspec(-1, v_off), kspec(0, v_off), kspec(1, v_off),
            pl.BlockSpec((n_ctx, HEAD_DIM), lambda b, n, k, s_: (b, k)),
            pl.BlockSpec((n_ctx, HEAD_DIM), lambda b, n, k, s_: (b, KV_HEADS + k)),
        ],
        out_specs=pl.BlockSpec((ATTN_BLOCK, q_cols), lambda b, n, k, s_: (b * nb + n, k)),
    )
    return pl.pallas_call(
        functools.partial(_attn_kernel, nb=nb),
        grid_spec=grid_spec,
        out_shape=jax.ShapeDtypeStruct((m, KV_HEADS * q_cols), BF16),
        compiler_params=_params("parallel", "parallel", "parallel"),
        name="window_attention",
    )(sink, qk, qk, qk, qk, proj, proj, proj, kvx, kvx)


def _gating_kernel(u0_ref, u1_ref, v0_ref, v1_ref, gn_ref, w_ref, b_ref, o_ref, *, groups):
    half = groups // 2
    v = jnp.concatenate([v0_ref[...], v1_ref[...]], axis=-1).astype(F32)
    v = jax.nn.gelu(v)
    ms = jnp.mean(v * v, axis=-1, keepdims=True)
    vn = ((v * lax.rsqrt(ms + NORM_EPS)) * gn_ref[...]).astype(BF16)
    for g in range(groups):
        cs = slice(g * HEAD_DIM, (g + 1) * HEAD_DIM)
        mixed = jnp.dot(w_ref[g], vn[:, cs], preferred_element_type=F32) + b_ref[:, g:g + 1]
        u_ref = u0_ref if g < half else u1_ref
        us = slice((g % half) * HEAD_DIM, (g % half + 1) * HEAD_DIM)
        o_ref[:, cs] = (jax.nn.gelu(u_ref[:, us].astype(F32)) * mixed).astype(o_ref.dtype)


def spatial_gating(proj, gate_norm, spatial_w, spatial_b, u_col, width):
    m = proj.shape[0]
    groups = width // HEAD_DIM
    hw = width // 2
    base = u_col // hw
    assert u_col % hw == 0

    def col(k):
        return pl.BlockSpec((CHUNK, hw), lambda i: (i, base + k))

    return pl.pallas_call(
        functools.partial(_gating_kernel, groups=groups),
        grid=(m // CHUNK,),
        in_specs=[
            col(0), col(1), col(2), col(3),
            pl.BlockSpec((1, width), lambda i: (0, 0)),
            pl.BlockSpec((groups, CHUNK, CHUNK), lambda i: (0, 0, 0)),
            pl.BlockSpec((CHUNK, groups), lambda i: (0, 0)),
        ],
        out_specs=pl.BlockSpec((CHUNK, width), lambda i: (i, 0)),
        out_shape=jax.ShapeDtypeStruct((m, width), BF16),
        compiler_params=_params("parallel"),
        name="spatial_gating",
    )(proj, proj, proj, proj, gate_norm.reshape(1, width), spatial_w.astype(BF16), spatial_b.T)


def _resid_matmul_kernel(*refs, n_pairs, has_extra):
    a_refs = refs[:n_pairs]
    w_refs = refs[n_pairs:2 * n_pairs]
    pos = 2 * n_pairs
    extra_ref = refs[pos] if has_extra else None
    pos += int(has_extra)
    res_ref, gate_ref, o_ref = refs[pos], refs[pos + 1], refs[pos + 2]
    wb_refs = refs[pos + 3:]
    _cast_on_first_row_block(w_refs, wb_refs)
    acc = jnp.dot(a_refs[0][...], wb_refs[0][...], preferred_element_type=F32)
    for a_ref, wb_ref in zip(a_refs[1:], wb_refs[1:]):
        acc += jnp.dot(a_ref[...], wb_ref[...], preferred_element_type=F32)
    if has_extra:
        acc += extra_ref[...].astype(F32)
    o_ref[...] = res_ref[...] + gate_ref[...] * acc


def resid_matmul(a_list, w, layer, res, gate, seq, extra=None, *, tm=1024, tn=512, name="resid_matmul"):
    m, n = res.shape
    tm = min(tm, m, seq)
    assert m % tm == 0 and n % tn == 0 and seq % tm == 0
    assert sum(a.shape[1] for a in a_list) == w.shape[1] and len({a.shape[1] for a in a_list}) == 1
    k = a_list[0].shape[1]
    in_specs = [pl.BlockSpec((tm, k), lambda j, i: (i, 0)) for _ in a_list]
    in_specs += [_wspec(w, layer, tn, row_block=p, rows=k) for p in range(len(a_list))]
    args = list(a_list) + [w] * len(a_list)
    if extra is not None:
        in_specs.append(pl.BlockSpec((tm, tn), lambda j, i: (i, j)))
        args.append(extra)
    in_specs += [
        pl.BlockSpec((tm, tn), lambda j, i: (i, j)),
        pl.BlockSpec((None, 1, tn), lambda j, i: ((i * tm) // seq, 0, j)),
    ]
    args += [res, gate]
    return pl.pallas_call(
        functools.partial(_resid_matmul_kernel, n_pairs=len(a_list), has_extra=extra is not None),
        grid=(n // tn, m // tm),
        in_specs=in_specs,
        out_specs=pl.BlockSpec((tm, tn), lambda j, i: (i, j)),
        out_shape=jax.ShapeDtypeStruct((m, n), F32),
        scratch_shapes=[pltpu.VMEM((k, tn), BF16) for _ in a_list],
        compiler_params=_params("parallel", "arbitrary"),
        name=name,
    )(*args)


def _conv_kernel(a_ref, ap_ref, an_ref, g_ref, gp_ref, gn_ref, gb_ref,
                 c_ref, cp_ref, cn_ref, x_ref, xp_ref, xn_ref,
                 dw_ref, db_ref, cg_ref, sw_ref, yc_ref, yd_ref, z_ref, y_ref, ssq_ref, *, tiles_per_seq):
    tp, width = a_ref.shape
    halo = CONV_HALO
    t = pl.program_id(0) % tiles_per_seq
    keep_prev = (t > 0).astype(F32)
    keep_next = (t < tiles_per_seq - 1).astype(F32)
    lane_chunks = width // 128

    def glu(a, g):
        return a.astype(F32) * jax.nn.sigmoid(g.astype(F32))

    def prod(a, b):
        return a.astype(F32) * b.astype(F32)

    def fill(main, prev, nxt):
        z_ref[pl.ds(0, halo), :] = prev * keep_prev
        z_ref[pl.ds(halo, tp), :] = main
        z_ref[pl.ds(halo + tp, halo), :] = nxt * keep_next

    def conv(w_ref, taps, cs):
        pad = taps // 2
        acc = jnp.zeros((tp, 128), F32)
        for k in range(taps):
            acc = acc + z_ref[pl.ds(halo + k - pad, tp), cs] * w_ref[pl.ds(k, 1), cs]
        return acc

    fill(glu(a_ref[...], g_ref[...]), glu(ap_ref[...], gp_ref[...]), glu(an_ref[...], gn_ref[...]))
    ssq_ref[...] = jnp.zeros_like(ssq_ref)

    def conv_c(c, carry):
        cs = pl.ds(pl.multiple_of(c * 128, 128), 128)
        y = conv(dw_ref, CONV_WIDTH, cs) + db_ref[:, cs]
        y_ref[:, cs] = y
        ssq_ref[...] += y * y
        return carry
    lax.fori_loop(0, lane_chunks, conv_c, 0)
    inv = lax.rsqrt(jnp.sum(ssq_ref[...], axis=-1, keepdims=True) / width + NORM_EPS)

    def norm_c(c, carry):
        cs = pl.ds(pl.multiple_of(c * 128, 128), 128)
        yc_ref[:, cs] = _silu(y_ref[:, cs] * inv * cg_ref[:, cs]).astype(yc_ref.dtype)
        return carry
    lax.fori_loop(0, lane_chunks, norm_c, 0)

    fill(prod(c_ref[...], x_ref[...]), prod(cp_ref[...], xp_ref[...]), prod(cn_ref[...], xn_ref[...]))

    def conv_d(c, carry):
        cs = pl.ds(pl.multiple_of(c * 128, 128), 128)
        yd_ref[:, cs] = (gb_ref[:, cs].astype(F32) * conv(sw_ref, SHORT_WIDTH, cs)).astype(yd_ref.dtype)
        return carry
    lax.fori_loop(0, lane_chunks, conv_d, 0)


def conv_mixers(proj, dw_w, dw_b, conv_norm, short_w, seq, *, tp=256):
    m = proj.shape[0]
    width = proj.shape[1] // 5
    tp = min(tp, seq)
    assert seq % tp == 0 and tp % CONV_HALO == 0
    per = tp // CONV_HALO
    n_halo = m // CONV_HALO

    def main(cb):
        return pl.BlockSpec((tp, width), lambda i: (i, cb))

    def prev(cb):
        return pl.BlockSpec((CONV_HALO, width), lambda i: (jnp.maximum(i * per - 1, 0), cb))

    def nxt(cb):
        return pl.BlockSpec((CONV_HALO, width), lambda i: (jnp.minimum((i + 1) * per, n_halo - 1), cb))

    def full(arr):
        return pl.BlockSpec(arr.shape, lambda i: (0, 0))

    dw_b2, cn2 = dw_b.reshape(1, width), conv_norm.reshape(1, width)
    out = jax.ShapeDtypeStruct((m, width), BF16)
    return pl.pallas_call(
        functools.partial(_conv_kernel, tiles_per_seq=seq // tp),
        grid=(m // tp,),
        in_specs=[main(0), prev(0), nxt(0), main(1), prev(1), nxt(1), main(2),
                  main(3), prev(3), nxt(3), main(4), prev(4), nxt(4),
                  full(dw_w), full(dw_b2), full(cn2), full(short_w)],
        out_specs=[pl.BlockSpec((tp, width), lambda i: (i, 0))] * 2,
        out_shape=[out, out],
        scratch_shapes=[pltpu.VMEM((tp + 2 * CONV_HALO, width), F32), pltpu.VMEM((tp, width), F32),
                        pltpu.VMEM((tp, 128), F32)],
        compiler_params=_params("parallel"),
        name="conv_mixers",
    )(proj, proj, proj, proj, proj, proj, proj, proj, proj, proj, proj, proj, proj,
      dw_w, dw_b2, cn2, short_w)


def _route(logits, bias):
    t, e = logits.shape
    gsize = e // N_GROUPS
    neg = -jnp.inf
    scores = jax.nn.sigmoid(logits)
    biased = scores + bias
    lane = lax.broadcasted_iota(jnp.int32, (t, e), 1)
    big = jnp.int32(e)

    def rmax(x):
        return jnp.max(x, axis=-1, keepdims=True)

    def first_lane(mask):
        return jnp.min(jnp.where(mask, lane, big), axis=-1, keepdims=True)

    gscore = jnp.zeros((t, e), F32)
    for g in range(N_GROUPS):
        in_g = (lane >= g * gsize) & (lane < (g + 1) * gsize)
        xg = jnp.where(in_g, biased, neg)
        m1 = rmax(xg)
        n_top = jnp.sum(jnp.where(xg == m1, 1.0, 0.0), axis=-1, keepdims=True)
        m2 = jnp.where(n_top >= 2.0, m1, rmax(jnp.where(xg < m1, xg, neg)))
        gscore = jnp.where(in_g, m1 + m2, gscore)

    cand = jnp.where(lane % gsize == 0, gscore, neg)
    allowed = jnp.zeros((t, e), jnp.bool_)
    for _ in range(TOPK_GROUPS):
        idx = first_lane(cand == rmax(cand))
        allowed = allowed | ((lane >= idx) & (lane < idx + gsize))
        cand = jnp.where(lane == idx, neg, cand)

    masked = jnp.where(allowed, biased, neg)
    chosen = jnp.zeros((t, e), jnp.bool_)
    for _ in range(TOP_K):
        idx = first_lane(masked == rmax(masked))
        pick = lane == idx
        chosen = chosen | pick
        masked = jnp.where(pick, neg, masked)

    w = jnp.where(chosen, scores, 0.0)
    return w / jnp.sum(w, axis=-1, keepdims=True) * ROUTED_SCALE


def _ffn_prep_kernel(x_ref, g_ref, sc_ref, sh_ref, rw_ref, rb_ref, h_ref, gates_ref):
    e = rb_ref.shape[-1]
    h = _modnorm_rows(x_ref[...], g_ref[...], sc_ref[...], sh_ref[...])
    h_hi = h.astype(BF16)
    h_ref[...] = h_hi.astype(h_ref.dtype)
    h_lo = (h - h_hi.astype(F32)).astype(BF16)
    both = jnp.dot(h_hi, rw_ref[...], preferred_element_type=F32)
    logits = both[:, :e] + both[:, e:] + jnp.dot(h_lo, rw_ref[:, :e], preferred_element_type=F32)
    gates_ref[...] = _route(logits, rb_ref[...])


def ffn_prep(x, g, sc1p, sh, router_w, router_b, seq, *, tm=256):
    m, d = x.shape
    e = router_w.shape[1]
    tm = min(tm, m)
    rw_hi = router_w.astype(BF16)
    rw_lo = (router_w - rw_hi.astype(F32)).astype(BF16)
    rw = jnp.concatenate([rw_hi, rw_lo], axis=1)
    return pl.pallas_call(
        _ffn_prep_kernel,
        grid=(m // tm,),
        in_specs=[
            pl.BlockSpec((tm, d), lambda i: (i, 0)),
            pl.BlockSpec((1, d), lambda i: (0, 0)),
            pl.BlockSpec((None, 1, d), lambda i: ((i * tm) // seq, 0, 0)),
            pl.BlockSpec((None, 1, d), lambda i: ((i * tm) // seq, 0, 0)),
            pl.BlockSpec((d, 2 * e), lambda i: (0, 0)),
            pl.BlockSpec((1, e), lambda i: (0, 0)),
        ],
        out_specs=[pl.BlockSpec((tm, d), lambda i: (i, 0)), pl.BlockSpec((tm, e), lambda i: (i, 0))],
        out_shape=[jax.ShapeDtypeStruct((m, d), BF16), jax.ShapeDtypeStruct((m, e), F32)],
        compiler_params=_params("parallel"),
        name="ffn_prep",
    )(x, g, sc1p, sh, rw, router_b.reshape(1, e))


MOE_TB = 256
MOE_UNIT = 16
MOE_CHUNK = 256
MOE_TM = 256
UNITS_PER_CHUNK = MOE_CHUNK // MOE_UNIT


def _ceil_to(x, k):
    return (x + k - 1) // k * k


def _moe_sizes(m, e, tb):
    nb = m // tb
    pad_units = -(-(MOE_TM // MOE_UNIT - 1) // nb)
    rows_loc = _ceil_to(tb * TOP_K + e * (MOE_UNIT - 1) + e * MOE_UNIT * pad_units, MOE_CHUNK)
    rows_max = _ceil_to(m * TOP_K + nb * e * (MOE_UNIT - 1) + e * (MOE_TM - MOE_UNIT), MOE_TM)
    return rows_loc // MOE_UNIT, rows_max


def _moe_plan(gates, tb):
    m, e = gates.shape
    nb = m // tb
    umax, rows_max = _moe_sizes(m, e, tb)
    sel = (gates > 0).reshape(nb, tb, e)
    seli = sel.astype(jnp.int32)
    rank = jnp.where(sel, jnp.cumsum(seli, axis=1) - seli, -1)
    seg = _ceil_to(jnp.sum(seli, axis=1), MOE_UNIT)
    tot = jnp.sum(seg, axis=0)
    extra = (_ceil_to(tot, MOE_TM) - tot) // MOE_UNIT
    spread = extra[None, :] // nb + (jnp.arange(nb)[:, None] < extra[None, :] % nb)
    seg = seg + MOE_UNIT * spread
    loff = jnp.cumsum(seg, axis=1) - seg
    lend = loff + seg
    tot = jnp.sum(seg, axis=0)
    gstart = jnp.cumsum(tot) - tot
    goff = gstart[None, :] + jnp.cumsum(seg, axis=0) - seg
    n_units = jnp.sum(seg, axis=1) // MOE_UNIT

    urow = jnp.arange(umax) * MOE_UNIT
    inseg = (loff[:, None, :] <= urow[None, :, None]) & (urow[None, :, None] < lend[:, None, :])
    dst = (jnp.sum(jnp.where(inseg, (goff - loff)[:, None, :], 0), axis=-1) + urow[None, :]) // MOE_UNIT
    dst = jnp.where(jnp.any(inseg, axis=-1), dst, 0).astype(jnp.int32)

    tmax = rows_max // MOE_TM
    n_tiles = jnp.sum(tot) // MOE_TM
    tile = jnp.minimum(jnp.arange(tmax), n_tiles - 1)
    tile_exp = jnp.sum((gstart + tot)[None, :] <= (tile * MOE_TM)[:, None], axis=-1)
    first = jnp.concatenate([jnp.ones((1,), bool), tile_exp[1:] != tile_exp[:-1]])
    after = ((gstart + tot) // MOE_TM)[tile_exp]
    next_exp = jnp.where(after < n_tiles, tile_exp[jnp.minimum(after, tmax - 1)], -1)
    return dict(
        first=first.astype(jnp.int32), next_exp=next_exp.astype(jnp.int32),
        rank=rank.astype(BF16), rank_t=jnp.swapaxes(rank, 1, 2).astype(BF16),
        loff_row=loff.astype(F32).reshape(nb, 1, e), lend_row=lend.astype(F32).reshape(nb, 1, e),
        loff_col=loff.astype(F32).reshape(nb, e, 1), lend_col=lend.astype(F32).reshape(nb, e, 1),
        dst=dst.reshape(-1), n_units=n_units.astype(jnp.int32), umax=umax,
        tile=tile.astype(jnp.int32), tile_exp=tile_exp.astype(jnp.int32),
        n_tiles=n_tiles.astype(jnp.int32).reshape(1))


def _unit_copies(dst_ref, hbm_ref, buf_ref, sem_ref, *, umax, to_hbm):
    base = pl.program_id(0) * umax

    def copy(c, u):
        g = dst_ref[base + c * UNITS_PER_CHUNK + u]
        slot = c % 2
        loc = buf_ref.at[slot, pl.ds(u * MOE_UNIT, MOE_UNIT), :]
        glob = hbm_ref.at[pl.ds(pl.multiple_of(g * MOE_UNIT, MOE_UNIT), MOE_UNIT), :]
        src, dst = (loc, glob) if to_hbm else (glob, loc)
        return pltpu.make_async_copy(src, dst, sem_ref.at[slot])
    return copy


def _for_units(c, n_units, fn):
    for u in range(UNITS_PER_CHUNK):
        @pl.when(c * UNITS_PER_CHUNK + u < n_units)
        def _():
            fn(u)


def _dispatch_kernel(dst_ref, nun_ref, h_ref, rank_t_ref, loff_ref, lend_ref, rows_in_ref, xs_ref, buf_ref, sem_ref,
                     *, umax):
    del rows_in_ref
    n_units = nun_ref[pl.program_id(0)]
    n_chunks = (n_units + UNITS_PER_CHUNK - 1) // UNITS_PER_CHUNK
    copy = _unit_copies(dst_ref, xs_ref, buf_ref, sem_ref, umax=umax, to_hbm=True)
    n_exp = loff_ref.shape[-1]

    def chunk(c, carry):
        @pl.when(c >= 2)
        def _():
            _for_units(c - 2, n_units, lambda u: copy(c - 2, u).wait())
        r = (c * MOE_CHUNK + lax.broadcasted_iota(jnp.int32, (MOE_CHUNK, n_exp), 0)).astype(F32)
        loff = loff_ref[...]
        in_seg = (loff <= r) & (r < lend_ref[...])
        j = r[:, :1] - jnp.sum(jnp.where(in_seg, loff, 0.0), axis=-1, keepdims=True)
        rk = jnp.dot(jnp.where(in_seg, 1.0, 0.0).astype(BF16), rank_t_ref[...], preferred_element_type=F32)
        one_hot = jnp.where(rk == j, 1.0, 0.0).astype(BF16)
        buf_ref[c % 2] = jnp.dot(one_hot, h_ref[...], preferred_element_type=F32).astype(buf_ref.dtype)
        _for_units(c, n_units, lambda u: copy(c, u).start())
        return carry
    lax.fori_loop(0, n_chunks, chunk, 0)

    @pl.when(n_chunks >= 2)
    def _():
        _for_units(n_chunks - 2, n_units, lambda u: copy(n_chunks - 2, u).wait())
    _for_units(n_chunks - 1, n_units, lambda u: copy(n_chunks - 1, u).wait())


def moe_dispatch(h, plan, tb, rows_buf):
    m, d = h.shape
    e = plan["rank"].shape[-1]
    umax = plan["umax"]
    grid_spec = pltpu.PrefetchScalarGridSpec(
        num_scalar_prefetch=2,
        grid=(m // tb,),
        in_specs=[
            pl.BlockSpec((tb, d), lambda b, *_: (b, 0)),
            pl.BlockSpec((None, e, tb), lambda b, *_: (b, 0, 0)),
            pl.BlockSpec((None, 1, e), lambda b, *_: (b, 0, 0)),
            pl.BlockSpec((None, 1, e), lambda b, *_: (b, 0, 0)),
            pl.BlockSpec(memory_space=pl.ANY),
        ],
        out_specs=pl.BlockSpec(memory_space=pl.ANY),
        scratch_shapes=[pltpu.VMEM((2, MOE_CHUNK, d), h.dtype), pltpu.SemaphoreType.DMA((2,))],
    )
    return pl.pallas_call(
        functools.partial(_dispatch_kernel, umax=umax),
        grid_spec=grid_spec,
        out_shape=jax.ShapeDtypeStruct(rows_buf.shape, rows_buf.dtype),
        input_output_aliases={6: 0},
        compiler_params=_params("arbitrary"),
        name="moe_dispatch",
    )(plan["dst"], plan["n_units"], h, plan["rank_t"], plan["loff_row"], plan["lend_row"], rows_buf)


def _experts_kernel(tile_ref, exp_ref, first_ref, next_ref, nt_ref, x_ref, wg_hbm, wu_hbm, wd_hbm, o_ref,
                    stage_g, stage_u, stage_d, wb_g, wb_u, wb_d, sem_ref, *, layer):
    t = pl.program_id(0)
    stages = (stage_g, stage_u, stage_d)
    sources = (wg_hbm, wu_hbm, wd_hbm)

    def copies(e):
        return [pltpu.make_async_copy(src.at[layer, e], dst, sem_ref.at[k])
                for k, (src, dst) in enumerate(zip(sources, stages))]

    @pl.when(t < nt_ref[0])
    def _():
        e = exp_ref[t]

        @pl.when(first_ref[t] == 1)
        def _():
            @pl.when(t == 0)
            def _():
                for cp in copies(e):
                    cp.start()
            for cp in copies(e):
                cp.wait()
            for stage, wb in zip(stages, (wb_g, wb_u, wb_d)):
                step = math.gcd(CAST_ROWS, stage.shape[0])

                def body(r, carry, stage=stage, wb=wb, step=step):
                    rs = pl.ds(pl.multiple_of(r * step, step), step)
                    wb[rs, :] = stage[rs, :].astype(wb.dtype)
                    return carry
                lax.fori_loop(0, stage.shape[0] // step, body, 0)

            @pl.when(next_ref[t] >= 0)
            def _():
                for cp in copies(next_ref[t]):
                    cp.start()

        x = x_ref[...]
        hg = jnp.dot(x, wb_g[...], preferred_element_type=F32)
        hu = jnp.dot(x, wb_u[...], preferred_element_type=F32)
        hid = (_silu(hg) * hu).astype(BF16)
        o_ref[...] = jnp.dot(hid, wb_d[...], preferred_element_type=F32).astype(o_ref.dtype)


def moe_experts(xs, plan, w_gate, w_up, w_down, layer):
    rows, d = xs.shape
    f = w_gate.shape[-1]
    any_spec = pl.BlockSpec(memory_space=pl.ANY)
    grid_spec = pltpu.PrefetchScalarGridSpec(
        num_scalar_prefetch=5,
        grid=(rows // MOE_TM,),
        in_specs=[pl.BlockSpec((MOE_TM, d), lambda t, tile, *_: (tile[t], 0)), any_spec, any_spec, any_spec],
        out_specs=pl.BlockSpec((MOE_TM, d), lambda t, tile, *_: (tile[t], 0)),
        scratch_shapes=[pltpu.VMEM((d, f), F32), pltpu.VMEM((d, f), F32), pltpu.VMEM((f, d), F32),
                        pltpu.VMEM((d, f), BF16), pltpu.VMEM((d, f), BF16), pltpu.VMEM((f, d), BF16),
                        pltpu.SemaphoreType.DMA((3,))],
    )
    return pl.pallas_call(
        functools.partial(_experts_kernel, layer=layer),
        grid_spec=grid_spec,
        out_shape=jax.ShapeDtypeStruct((rows, d), xs.dtype),
        input_output_aliases={5: 0},
        compiler_params=_params("arbitrary"),
        name="moe_experts",
    )(plan["tile"], plan["tile_exp"], plan["first"], plan["next_exp"], plan["n_tiles"], xs, w_gate, w_up, w_down)


def _combine_kernel(dst_ref, nun_ref, rank_ref, gates_ref, loff_ref, lend_ref, y_ref, o_ref, buf_ref, sem_ref,
                    *, umax):
    n_units = nun_ref[pl.program_id(0)]
    n_chunks = (n_units + UNITS_PER_CHUNK - 1) // UNITS_PER_CHUNK
    copy = _unit_copies(dst_ref, y_ref, buf_ref, sem_ref, umax=umax, to_hbm=False)
    n_exp = loff_ref.shape[0]

    @pl.when(pl.program_id(0) == 0)
    def _():
        buf_ref[...] = jnp.zeros_like(buf_ref)

    _for_units(0, n_units, lambda u: copy(0, u).start())
    o_ref[...] = jnp.zeros_like(o_ref)
    rank = rank_ref[...]
    gates = gates_ref[...].astype(BF16)

    def chunk(c, carry):
        @pl.when(c + 1 < n_chunks)
        def _():
            _for_units(c + 1, n_units, lambda u: copy(c + 1, u).start())
        _for_units(c, n_units, lambda u: copy(c, u).wait())
        r = (c * MOE_CHUNK + lax.broadcasted_iota(jnp.int32, (n_exp, MOE_CHUNK), 1)).astype(F32)
        loff = loff_ref[...]
        in_seg = (loff <= r) & (r < lend_ref[...])
        j = r[:1, :] - jnp.sum(jnp.where(in_seg, loff, 0.0), axis=0, keepdims=True)
        seg_t = jnp.where(in_seg, 1.0, 0.0).astype(BF16)
        rk = jnp.dot(rank, seg_t, preferred_element_type=F32)
        gt = jnp.dot(gates, seg_t, preferred_element_type=F32)
        weights = jnp.where(rk == j, gt, 0.0).astype(BF16)
        o_ref[...] += jnp.dot(weights, buf_ref[c % 2], preferred_element_type=F32)
        return carry
    lax.fori_loop(0, n_chunks, chunk, 0)


def moe_combine(y, gates, plan, tb):
    m, e = gates.shape
    d = y.shape[1]
    umax = plan["umax"]
    grid_spec = pltpu.PrefetchScalarGridSpec(
        num_scalar_prefetch=2,
        grid=(m // tb,),
        in_specs=[
            pl.BlockSpec((None, tb, e), lambda b, *_: (b, 0, 0)),
            pl.BlockSpec((tb, e), lambda b, *_: (b, 0)),
            pl.BlockSpec((None, e, 1), lambda b, *_: (b, 0, 0)),
            pl.BlockSpec((None, e, 1), lambda b, *_: (b, 0, 0)),
            pl.BlockSpec(memory_space=pl.ANY),
        ],
        out_specs=pl.BlockSpec((tb, d), lambda b, *_: (b, 0)),
        scratch_shapes=[pltpu.VMEM((2, MOE_CHUNK, d), y.dtype), pltpu.SemaphoreType.DMA((2,))],
    )
    return pl.pallas_call(
        functools.partial(_combine_kernel, umax=umax),
        grid_spec=grid_spec,
        out_shape=jax.ShapeDtypeStruct((m, d), F32),
        compiler_params=_params("arbitrary"),
        name="moe_combine",
    )(plan["dst"], plan["n_units"], plan["rank"], gates, plan["loff_col"], plan["lend_col"], y)


def moe_routed(h, gates, w_gate, w_up, w_down, layer, rows_buf):
    tb = min(MOE_TB, h.shape[0])
    plan = _moe_plan(gates, tb)
    xs = moe_dispatch(h, plan, tb, rows_buf)
    y = moe_experts(xs, plan, w_gate, w_up, w_down, layer)
    return moe_combine(y, gates, plan, tb), y


def moe_rows_buffer(m, d, n_exp):
    return jnp.zeros((_moe_sizes(m, n_exp, min(MOE_TB, m))[1], d), BF16)


def _glu_matmul_kernel(x_ref, wg_ref, wu_ref, o_ref, wbg_ref, wbu_ref):
    _cast_on_first_row_block([wg_ref, wu_ref], [wbg_ref, wbu_ref])
    x = x_ref[...]
    hg = jnp.dot(x, wbg_ref[...], preferred_element_type=F32)
    hu = jnp.dot(x, wbu_ref[...], preferred_element_type=F32)
    o_ref[...] = (_silu(hg) * hu).astype(o_ref.dtype)


def glu_matmul(h, wg, wu, layer, *, tm=512, tn=384):
    m, d = h.shape
    f = wg.shape[2]
    tm = min(tm, m)
    assert m % tm == 0 and f % tn == 0
    return pl.pallas_call(
        _glu_matmul_kernel,
        grid=(f // tn, m // tm),
        in_specs=[pl.BlockSpec((tm, d), lambda j, i: (i, 0)), _wspec(wg, layer, tn), _wspec(wu, layer, tn)],
        out_specs=pl.BlockSpec((tm, tn), lambda j, i: (i, j)),
        out_shape=jax.ShapeDtypeStruct((m, f), BF16),
        scratch_shapes=[pltpu.VMEM((d, tn), BF16), pltpu.VMEM((d, tn), BF16)],
        compiler_params=_params("parallel", "arbitrary"),
        name="shared_glu",
    )(h, wg, wu)


def _rmsnorm_kernel(x_ref, g_ref, o_ref):
    x = x_ref[...]
    ms = jnp.mean(x * x, axis=-1, keepdims=True)
    o_ref[...] = (x * lax.rsqrt(ms + NORM_EPS)) * g_ref[...]


def rms_norm(x, g, *, tm=256):
    m, d = x.shape
    tm = min(tm, m)
    return pl.pallas_call(
        _rmsnorm_kernel,
        grid=(m // tm,),
        in_specs=[pl.BlockSpec((tm, d), lambda i: (i, 0)), pl.BlockSpec((1, d), lambda i: (0, 0))],
        out_specs=pl.BlockSpec((tm, d), lambda i: (i, 0)),
        out_shape=jax.ShapeDtypeStruct((m, d), F32),
        compiler_params=_params("parallel"),
        name="final_norm",
    )(x, g.reshape(1, d))


def kernel(x, c, ctx, c_ctx, ada_w, ada_b, norm_mix, norm_ffn, ev_w_in, ev_w_out, ev_sink, ev_gate_norm, ev_spatial_w, ev_spatial_b, od_w_in, od_w_out, od_dw_w, od_dw_b, od_conv_norm, od_short_w, router_w, router_b, exp_w_gate, exp_w_up, exp_w_down, sh_w_gate, sh_w_up, sh_w_down, final_norm):
    b, s, d = x.shape
    n_ctx = ctx.shape[1]
    depth = ada_w.shape[0]
    m = b * s
    a_width = d // 2
    kv_width = KV_HEADS * HEAD_DIM
    assert a_width == KV_HEADS * Q_GROUP * HEAD_DIM and s % ATTN_BLOCK == 0 and n_ctx % 16 == 0

    c_rows = jnp.zeros((8, d), F32).at[:b].set(_silu(c)).at[b].set(_silu(c_ctx)).astype(BF16)
    mod = ada_modulation(c_rows, ada_w, ada_b)
    tables = rope_tables(s)
    xf = x.reshape(m, d)
    rows_buf = moe_rows_buffer(m, d, router_w.shape[-1])

    for i in range(depth):
        j = i // 2
        sh1, sc1, g1, sh2, sc2, g2 = [mod[i, :b, k * d:(k + 1) * d].reshape(b, 1, d) for k in range(6)]
        gm = norm_mix[i].reshape(1, d)
        h = modnorm(xf, gm, 1.0 + sc1, sh1, s)
        if i % 2 == 0:
            proj = proj_matmul(h, ev_w_in, j, name="even_in")
            csh1 = mod[i, b, 0:d].reshape(1, 1, d)
            csc1 = mod[i, b, d:2 * d].reshape(1, 1, d)
            hc = modnorm(ctx.reshape(b * n_ctx, d), gm, 1.0 + csc1, csh1, b * n_ctx)
            kvx = proj_matmul(hc, ev_w_in, j, n=2 * kv_width, col0=a_width // PROJ_TN, name="ctx_kv")
            qk = apply_rope(proj, tables, s, a_width + kv_width)
            attn = window_attention(qk, proj, kvx, ev_sink[j], b, s, n_ctx)
            gated = spatial_gating(proj, ev_gate_norm[j], ev_spatial_w[j], ev_spatial_b[j],
                                   a_width + 2 * kv_width, d - a_width)
            mix, w_out = [attn, gated], ev_w_out
        else:
            proj = proj_matmul(h, od_w_in, j, name="odd_in")
            mix = list(conv_mixers(proj, od_dw_w[j], od_dw_b[j], od_conv_norm[j], od_short_w[j], s))
            w_out = od_w_out
        xf = resid_matmul(mix, w_out, j, xf, g1, s, name="mix_out")

        hf, gates = ffn_prep(xf, norm_ffn[i].reshape(1, d), 1.0 + sc2, sh2, router_w[i], router_b[i], s)
        routed, rows_buf = moe_routed(hf, gates, exp_w_gate, exp_w_up, exp_w_down, i, rows_buf)
        shared = glu_matmul(hf, sh_w_gate, sh_w_up, i)
        xf = resid_matmul([shared], sh_w_down, i, xf, g2, s, extra=routed, name="ffn_out")

    return rms_norm(xf, final_norm).reshape(b, s, d)
```
